```python
import math
import jax, jax.numpy as jnp
from jax import lax
import numpy as np

D_MODEL = 4096
BATCH = 1
SEQ = 8192
DEPTH = 2

CTX_LEN = 256
GRID_W = 64
N_EVEN = (DEPTH + 1) // 2
N_ODD = DEPTH // 2
RMS_EPS = 1e-6
LN_EPS = 1e-5
A_WIDTH = D_MODEL // 2
A_HEAD_DIM = 64
A_HEADS = A_WIDTH // (2 * A_HEAD_DIM)
B_WIDTH = D_MODEL - A_WIDTH
CONV_WIDTH = 31
Q_BLOCK = 128
ROPE_BASE = 10000.0
ROPE_AXIS_DIM = A_HEAD_DIM // 2
EV_V0 = A_WIDTH
EV_Q0 = 2 * A_WIDTH
EV_B0 = 3 * A_WIDTH
EV_IN = 3 * A_WIDTH + 2 * B_WIDTH
S5_WIDTH = D_MODEL // 4
S5_GROUP = 16
S5_GROUPS = S5_WIDTH // S5_GROUP
S5_STATE = 64
M2_INNER = D_MODEL - S5_WIDTH
M2_HEAD_DIM = 64
M2_HEADS = M2_INNER // M2_HEAD_DIM
M2_GROUPS = 8
M2_STATE = 128
M2_CONV = 5
M2_CONV_DIM = M2_INNER + 2 * M2_GROUPS * M2_STATE
SSD_CHUNK = 128
OD_DT0 = S5_WIDTH + M2_CONV_DIM
OD_Z0 = OD_DT0 + 2 * M2_HEADS
OD_IN = OD_Z0 + M2_INNER
FFN_HIDDEN = -(-8 * D_MODEL // (3 * 256)) * 256

kernel_name = 'hybrid_diffattn_conformer_s5_ssd_trunk'


def _rmsnorm(x, w, eps=RMS_EPS):
    xf = x.astype(jnp.float32)
    y = xf * lax.rsqrt(jnp.mean(xf * xf, axis=-1, keepdims=True) + eps)
    return (y * w.astype(jnp.float32)).astype(x.dtype)


def _layernorm(x, w, b, eps=LN_EPS):
    xf = x.astype(jnp.float32)
    mu = jnp.mean(xf, axis=-1, keepdims=True)
    var = jnp.mean(jnp.square(xf - mu), axis=-1, keepdims=True)
    y = (xf - mu) * lax.rsqrt(var + eps)
    return (y * w.astype(jnp.float32) + b.astype(jnp.float32)).astype(x.dtype)


def _modulate(x, shift, scale):
    return x * (1 + scale) + shift


def _swiglu(x, w1, w3, w2):
    return (jax.nn.silu(x @ w1) * (x @ w3)) @ w2


def _dwconv(u, w, bias):
    k = w.shape[0]
    pad = (k - 1) // 2
    y = lax.conv_general_dilated(u, w[:, None, :], window_strides=(1,), padding=[(pad, pad)],
                                 dimension_numbers=('NWC', 'WIO', 'NWC'),
                                 feature_group_count=u.shape[-1])
    return y + bias


def _axial_rope_tables(n_rows):
    rows = jnp.repeat(jnp.arange(n_rows), GRID_W).astype(jnp.float32)
    cols = jnp.tile(jnp.arange(GRID_W), n_rows).astype(jnp.float32)
    inv = jnp.power(ROPE_BASE, -jnp.arange(0, ROPE_AXIS_DIM, 2, dtype=jnp.float32) / ROPE_AXIS_DIM)
    ang_r = rows[:, None, None] * inv
    ang_c = cols[:, None, None] * inv
    return (jnp.cos(ang_r), jnp.sin(ang_r), jnp.cos(ang_c), jnp.sin(ang_c))


def _rotate_half(x, cos, sin):
    x1, x2 = jnp.split(x, 2, axis=-1)
    return jnp.concatenate([x1 * cos - x2 * sin, x2 * cos + x1 * sin], axis=-1)


def _rope_2d(x, rope):
    cos_r, sin_r, cos_c, sin_c = [r.astype(x.dtype) for r in rope]
    x_row, x_col = jnp.split(x, 2, axis=-1)
    return jnp.concatenate([_rotate_half(x_row, cos_r, sin_r), _rotate_half(x_col, cos_c, sin_c)], axis=-1)


def _diff_softmax_mix(q, k, v, lam_full):
    s = jnp.einsum('bmqd,bmkd->bmqk', q, k).astype(jnp.float32)
    p = jax.nn.softmax(s, axis=-1)
    bsz, maps, nq, nk = p.shape
    p = p.reshape(bsz, maps // 2, 2, nq, nk)
    w = p[:, :, 0] - lam_full * p[:, :, 1]
    return jnp.einsum('bhqk,bhkv->bhqv', w.astype(v.dtype), v)


def _diff_attention_blocked(q, k, v, lam_full):
    bsz, maps, t, d = q.shape
    nblk = t // Q_BLOCK
    qb = jnp.moveaxis(q.reshape(bsz, maps, nblk, Q_BLOCK, d), 2, 0)
    out = lax.map(lambda qq: _diff_softmax_mix(qq, k, v, lam_full), qb)
    return jnp.moveaxis(out, 0, 2).reshape(bsz, maps // 2, t, v.shape[-1])


def _diff_heads_merge(o, subln_w, lam_init):
    o = _rmsnorm(o, subln_w, eps=1e-5) * (1 - lam_init)
    bsz, h, t, e = o.shape
    return o.transpose(0, 2, 1, 3).reshape(bsz, t, h * e)


def _conformer_conv(u, conv_w, conv_b, ln_w, ln_b):
    a, g = jnp.split(u, 2, axis=-1)
    y = _dwconv(a * jax.nn.sigmoid(g), conv_w, conv_b)
    return jax.nn.silu(_layernorm(y, ln_w, ln_b))


def _even_mixer(a_lat, a_ctx, w_in, w_out, lam, subln_w, conv_w, conv_b, ln_w, ln_b, rope, layer_idx, need_ctx):
    bsz, n_lat, _ = a_lat.shape
    n_ctx = a_ctx.shape[1]
    p_lat = a_lat @ w_in
    p_ctx = a_ctx @ (w_in if need_ctx else w_in[:, :EV_Q0])
    q_scale = A_HEAD_DIM ** -0.5
    k_lat = _rope_2d(p_lat[..., :EV_V0].reshape(bsz, n_lat, 2 * A_HEADS, A_HEAD_DIM), rope)
    q_lat = _rope_2d(p_lat[..., EV_Q0:EV_B0].reshape(bsz, n_lat, 2 * A_HEADS, A_HEAD_DIM), rope) * q_scale
    v_lat = p_lat[..., EV_V0:EV_Q0].reshape(bsz, n_lat, A_HEADS, 2 * A_HEAD_DIM)
    k_ctx = p_ctx[..., :EV_V0].reshape(bsz, n_ctx, 2 * A_HEADS, A_HEAD_DIM)
    v_ctx = p_ctx[..., EV_V0:EV_Q0].reshape(bsz, n_ctx, A_HEADS, 2 * A_HEAD_DIM)
    k_all = jnp.concatenate([k_ctx, k_lat], axis=1).transpose(0, 2, 1, 3)
    v_all = jnp.concatenate([v_ctx, v_lat], axis=1).transpose(0, 2, 1, 3)
    lam_f = lam.astype(jnp.float32)
    lam_init = 0.8 - 0.6 * math.exp(-0.3 * layer_idx)
    lam_full = jnp.exp(jnp.sum(lam_f[0] * lam_f[1])) - jnp.exp(jnp.sum(lam_f[2] * lam_f[3])) + lam_init
    o_lat = _diff_attention_blocked(q_lat.transpose(0, 2, 1, 3), k_all, v_all, lam_full)
    y_lat = jnp.concatenate([_diff_heads_merge(o_lat, subln_w, lam_init),
                             _conformer_conv(p_lat[..., EV_B0:], conv_w, conv_b, ln_w, ln_b)], axis=-1) @ w_out
    if not need_ctx:
        return y_lat, None
    q_ctx = p_ctx[..., EV_Q0:EV_B0].reshape(bsz, n_ctx, 2 * A_HEADS, A_HEAD_DIM) * q_scale
    o_ctx = _diff_softmax_mix(q_ctx.transpose(0, 2, 1, 3), k_ctx.transpose(0, 2, 1, 3),
                              v_ctx.transpose(0, 2, 1, 3), lam_full)
    y_ctx = jnp.concatenate([_diff_heads_merge(o_ctx, subln_w, lam_init),
                             _conformer_conv(p_ctx[..., EV_B0:], conv_w, conv_b, ln_w, ln_b)], axis=-1) @ w_out
    return y_lat, y_ctx


def _linear_recurrence(e1, e2):
    a1, b1 = e1
    a2, b2 = e2
    return a1 * a2, a2 * b1 + b2


def _s5_states(u_c, lam_bar, b_bar, h0):
    bu = jnp.einsum('btgh,gph->btgp', u_c, b_bar)
    if h0 is not None:
        bu = bu.at[:, 0].add(lam_bar * h0)
    a = jnp.broadcast_to(lam_bar, bu.shape)
    _, s = lax.associative_scan(_linear_recurrence, (a, bu), axis=1)
    return s


def _s5_branch(u_lat, u_ctx, lam_re, lam_im, log_step, b_re, b_im, c_re, c_im, d_skip, glu_w, glu_b, need_ctx):
    f32 = jnp.float32
    lam = lax.complex(lam_re.astype(f32), lam_im.astype(f32))
    lam_bar = jnp.exp(lam * jnp.exp(log_step.astype(f32))[..., None])
    b_mat = lax.complex(b_re.astype(f32), b_im.astype(f32))
    b_bar = ((lam_bar - 1.0) / lam)[..., None] * b_mat
    c_mat = lax.complex(c_re.astype(f32), c_im.astype(f32))

    def grouped(u):
        return u.astype(f32).reshape(u.shape[0], u.shape[1], S5_GROUPS, S5_GROUP).astype(jnp.complex64)

    def rev(u):
        return u[:, ::-1]

    uc_ctx, uc_lat = grouped(u_ctx), grouped(u_lat)
    s_cf = _s5_states(uc_ctx, lam_bar[0], b_bar[0], None)
    s_cb = _s5_states(rev(uc_ctx), lam_bar[1], b_bar[1], None)
    s_lf = _s5_states(uc_lat, lam_bar[0], b_bar[0], s_cf[:, -1])
    s_lb = _s5_states(rev(uc_lat), lam_bar[1], b_bar[1], s_cb[:, -1])

    def finish(s_f, s_b_rev, u):
        bsz, n = u.shape[:2]
        y = (jnp.einsum('btgp,ghp->btgh', s_f, c_mat[0])
             + rev(jnp.einsum('btgp,ghp->btgh', s_b_rev, c_mat[1]))).real
        y = y.reshape(bsz, n, S5_WIDTH) + d_skip.astype(f32) * u.astype(f32)
        g = jax.nn.gelu(y).astype(u.dtype)
        return g * jax.nn.sigmoid(g @ glu_w + glu_b)

    y_lat = finish(s_lf, s_lb, u_lat)
    y_ctx = finish(s_cf, s_cb, u_ctx) if need_ctx else None
    return y_lat, y_ctx


def _segsum(a):
    cs = jnp.cumsum(a, axis=-1)
    T = a.shape[-1]
    diff = cs[..., :, None] - cs[..., None, :]
    return jnp.where(jnp.tril(jnp.ones((T, T), dtype=bool)), diff, -jnp.inf)


def _ssd_chunked(x, dt, a, bm, cm, h0, with_output):
    f32 = jnp.float32
    bsz, t, h, p = x.shape
    g, n = bm.shape[2], bm.shape[3]
    r = h // g
    nc, l = t // SSD_CHUNK, SSD_CHUNK
    xs = (x.astype(f32) * dt[..., None]).reshape(bsz, nc, l, g, r, p)
    da = (dt * a).reshape(bsz, nc, l, g, r).transpose(0, 3, 4, 1, 2)
    bs = bm.astype(f32).reshape(bsz, nc, l, g, n)
    cs = cm.astype(f32).reshape(bsz, nc, l, g, n)
    da_cum = jnp.cumsum(da, axis=-1)
    decay_states = jnp.exp(da_cum[..., -1:] - da_cum)
    states = jnp.einsum('bclgn,bgrcl,bclgrp->bcgrpn', bs, decay_states, xs)
    states = jnp.concatenate([h0.astype(f32).reshape(bsz, 1, g, r, p, n), states], axis=1)
    chunk_tot = jnp.pad(da_cum[..., -1], ((0, 0), (0, 0), (0, 0), (1, 0)))
    decay_chunk = jnp.exp(_segsum(chunk_tot))
    new_states = jnp.einsum('bgrzc,bcgrpn->bzgrpn', decay_chunk, states)
    final = new_states[:, -1].reshape(bsz, h, p, n)
    if not with_output:
        return None, final
    L = jnp.exp(_segsum(da))
    y_diag = jnp.einsum('bclgn,bcsgn,bgrcls,bcsgrp->bclgrp', cs, bs, L, xs)
    y_off = jnp.einsum('bclgn,bcgrpn,bgrcl->bclgrp', cs, new_states[:, :-1], jnp.exp(da_cum))
    return (y_diag + y_off).reshape(bsz, t, h, p), final


def _ssd_branch(xbc_lat, xbc_ctx, dt_lat, dt_ctx, z_lat, z_ctx, conv_w, conv_b, a_log, dt_bias, d_skip, norm_w,
                need_ctx):
    f32 = jnp.float32
    a = -jnp.exp(a_log.astype(f32))

    def prep(xbc, dt_raw):
        bsz, n, _ = xbc.shape
        xbc = jax.nn.silu(_dwconv(xbc, conv_w, conv_b))
        xs = xbc[..., :M2_INNER].reshape(bsz, n, M2_HEADS, M2_HEAD_DIM)
        bm = xbc[..., M2_INNER:M2_INNER + M2_GROUPS * M2_STATE].reshape(bsz, n, M2_GROUPS, M2_STATE)
        cm = xbc[..., M2_INNER + M2_GROUPS * M2_STATE:].reshape(bsz, n, M2_GROUPS, M2_STATE)
        dt = jax.nn.softplus(dt_raw.astype(f32).reshape(bsz, n, 2, M2_HEADS) + dt_bias.astype(f32))
        return xs, bm, cm, dt

    def rev(u):
        return u[:, ::-1]

    xs_c, b_c, c_c, dt_c = prep(xbc_ctx, dt_ctx)
    xs_l, b_l, c_l, dt_l = prep(xbc_lat, dt_lat)
    h0 = jnp.zeros((xs_c.shape[0], M2_HEADS, M2_HEAD_DIM, M2_STATE), f32)
    y_cf, h_f = _ssd_chunked(xs_c, dt_c[:, :, 0], a[0], b_c, c_c, h0, need_ctx)
    y_cb, h_b = _ssd_chunked(rev(xs_c), rev(dt_c[:, :, 1]), a[1], rev(b_c), rev(c_c), h0, need_ctx)
    y_lf, _ = _ssd_chunked(xs_l, dt_l[:, :, 0], a[0], b_l, c_l, h_f, True)
    y_lb, _ = _ssd_chunked(rev(xs_l), rev(dt_l[:, :, 1]), a[1], rev(b_l), rev(c_l), h_b, True)

    def finish(y_f, y_b_rev, xs, z):
        bsz, n = z.shape[:2]
        y = y_f + rev(y_b_rev) + d_skip.astype(f32)[:, None] * xs.astype(f32)
        y = y.reshape(bsz, n, M2_INNER) * jax.nn.silu(z.astype(f32))
        y = y.reshape(bsz, n, M2_GROUPS, M2_INNER // M2_GROUPS)
        y = y * lax.rsqrt(jnp.mean(y * y, axis=-1, keepdims=True) + RMS_EPS)
        return (y.reshape(bsz, n, M2_INNER) * norm_w.astype(f32)).astype(z.dtype)

    y_lat = finish(y_lf, y_lb, xs_l, z_lat)
    y_ctx = finish(y_cf, y_cb, xs_c, z_ctx) if need_ctx else None
    return y_lat, y_ctx


def _odd_mixer(a_lat, a_ctx, w_in, w_out, lam_re, lam_im, log_step, b_re, b_im, c_re, c_im, s5_d, glu_w, glu_b,
               conv_w, conv_b, a_log, dt_bias, m2_d, m2_norm_w, need_ctx):
    p_lat = a_lat @ w_in
    p_ctx = a_ctx @ (w_in if need_ctx else w_in[:, :OD_Z0])
    s5_lat, s5_ctx = _s5_branch(p_lat[..., :S5_WIDTH], p_ctx[..., :S5_WIDTH], lam_re, lam_im, log_step,
                                b_re, b_im, c_re, c_im, s5_d, glu_w, glu_b, need_ctx)
    ssd_lat, ssd_ctx = _ssd_branch(p_lat[..., S5_WIDTH:OD_DT0], p_ctx[..., S5_WIDTH:OD_DT0],
                                   p_lat[..., OD_DT0:OD_Z0], p_ctx[..., OD_DT0:OD_Z0],
                                   p_lat[..., OD_Z0:], p_ctx[..., OD_Z0:] if need_ctx else None,
                                   conv_w, conv_b, a_log, dt_bias, m2_d, m2_norm_w, need_ctx)
    y_lat = jnp.concatenate([s5_lat, ssd_lat], axis=-1) @ w_out
    y_ctx = (jnp.concatenate([s5_ctx, ssd_ctx], axis=-1) @ w_out) if need_ctx else None
    return y_lat, y_ctx


def setup_inputs(seed: int = 0) -> dict:
    key = jax.random.key(seed)
    ks = iter(jax.random.split(key, 48))
    f32 = jnp.float32
    D = D_MODEL

    def nrm(shape, scale):
        return scale * jax.random.normal(next(ks), shape, f32)

    def unif(shape, lo, hi):
        return jax.random.uniform(next(ks), shape, f32, lo, hi)

    x = nrm((BATCH, SEQ, D), 1.0)
    c = nrm((BATCH, D), 1.0)
    ctx = nrm((BATCH, CTX_LEN, D), 1.0)
    c_ctx = nrm((D,), 1.0)
    ada_w = nrm((DEPTH, D, 6 * D), 0.5 * D ** -0.5)
    ada_b = nrm((DEPTH, 6 * D), 0.02)
    norm_w = 1.0 + nrm((DEPTH, 2, D), 0.02)
    ffn_w1 = nrm((DEPTH, D, FFN_HIDDEN), D ** -0.5)
    ffn_w3 = nrm((DEPTH, D, FFN_HIDDEN), D ** -0.5)
    ffn_w2 = nrm((DEPTH, FFN_HIDDEN, D), FFN_HIDDEN ** -0.5)
    ev_w_in = nrm((N_EVEN, D, EV_IN), D ** -0.5)
    ev_w_out = nrm((N_EVEN, A_WIDTH + B_WIDTH, D), (A_WIDTH + B_WIDTH) ** -0.5)
    ev_lambda = nrm((N_EVEN, 4, A_HEAD_DIM), 0.1)
    ev_subln_w = 1.0 + nrm((N_EVEN, 2 * A_HEAD_DIM), 0.02)
    ev_conv_w = nrm((N_EVEN, CONV_WIDTH, B_WIDTH), CONV_WIDTH ** -0.5)
    ev_conv_b = nrm((N_EVEN, B_WIDTH), 0.02)
    ev_ln_w = 1.0 + nrm((N_EVEN, B_WIDTH), 0.02)
    ev_ln_b = nrm((N_EVEN, B_WIDTH), 0.02)
    od_w_in = nrm((N_ODD, D, OD_IN), D ** -0.5)
    od_w_out = nrm((N_ODD, S5_WIDTH + M2_INNER, D), (S5_WIDTH + M2_INNER) ** -0.5)
    s5_lam_re = -0.5 + nrm((N_ODD, 2, S5_GROUPS, S5_STATE), 0.01)
    s5_lam_im = math.pi * jnp.arange(S5_STATE, dtype=f32) + nrm((N_ODD, 2, S5_GROUPS, S5_STATE), 0.01)
    s5_log_step = unif((N_ODD, 2, S5_GROUPS), math.log(1e-3), math.log(1e-1))
    s5_b_re = nrm((N_ODD, S5_GROUPS, S5_STATE, S5_GROUP), (2 * S5_GROUP) ** -0.5)
    s5_b_im = nrm((N_ODD, S5_GROUPS, S5_STATE, S5_GROUP), (2 * S5_GROUP) ** -0.5)
    s5_c_re = nrm((N_ODD, 2, S5_GROUPS, S5_GROUP, S5_STATE), (2 * S5_STATE) ** -0.5)
    s5_c_im = nrm((N_ODD, 2, S5_GROUPS, S5_GROUP, S5_STATE), (2 * S5_STATE) ** -0.5)
    s5_d = nrm((N_ODD, S5_WIDTH), 0.5)
    s5_glu_w = nrm((N_ODD, S5_WIDTH, S5_WIDTH), S5_WIDTH ** -0.5)
    s5_glu_b = nrm((N_ODD, S5_WIDTH), 0.02)
    m2_conv_w = nrm((N_ODD, M2_CONV, M2_CONV_DIM), M2_CONV ** -0.5)
    m2_conv_b = nrm((N_ODD, M2_CONV_DIM), 0.02)
    m2_a_log = jnp.log(unif((N_ODD, 2, M2_HEADS), 1.0, 16.0))
    dt0 = jnp.exp(unif((N_ODD, 2, M2_HEADS), math.log(1e-3), math.log(1e-1)))
    m2_dt_bias = dt0 + jnp.log(-jnp.expm1(-dt0))
    m2_d = 1.0 + nrm((N_ODD, M2_HEADS), 0.02)
    m2_norm_w = 1.0 + nrm((N_ODD, M2_INNER), 0.02)
    final_norm_w = 1.0 + nrm((D,), 0.02)
    return {'x': x, 'c': c, 'ctx': ctx, 'c_ctx': c_ctx, 'ada_w': ada_w, 'ada_b': ada_b, 'norm_w': norm_w,
            'ffn_w1': ffn_w1, 'ffn_w3': ffn_w3, 'ffn_w2': ffn_w2,
            'ev_w_in': ev_w_in, 'ev_w_out': ev_w_out, 'ev_lambda': ev_lambda, 'ev_subln_w': ev_subln_w,
            'ev_conv_w': ev_conv_w, 'ev_conv_b': ev_conv_b, 'ev_ln_w': ev_ln_w, 'ev_ln_b': ev_ln_b,
            'od_w_in': od_w_in, 'od_w_out': od_w_out, 's5_lam_re': s5_lam_re, 's5_lam_im': s5_lam_im,
            's5_log_step': s5_log_step, 's5_b_re': s5_b_re, 's5_b_im': s5_b_im, 's5_c_re': s5_c_re,
            's5_c_im': s5_c_im, 's5_d': s5_d, 's5_glu_w': s5_glu_w, 's5_glu_b': s5_glu_b,
            'm2_conv_w': m2_conv_w, 'm2_conv_b': m2_conv_b, 'm2_a_log': m2_a_log, 'm2_dt_bias': m2_dt_bias,
            'm2_d': m2_d, 'm2_norm_w': m2_norm_w, 'final_norm_w': final_norm_w}


def reference(x, c, ctx, c_ctx, ada_w, ada_b, norm_w, ffn_w1, ffn_w3, ffn_w2,
              ev_w_in, ev_w_out, ev_lambda, ev_subln_w, ev_conv_w, ev_conv_b, ev_ln_w, ev_ln_b,
              od_w_in, od_w_out, s5_lam_re, s5_lam_im, s5_log_step, s5_b_re, s5_b_im, s5_c_re, s5_c_im,
              s5_d, s5_glu_w, s5_glu_b, m2_conv_w, m2_conv_b, m2_a_log, m2_dt_bias, m2_d, m2_norm_w,
              final_norm_w):
    n_rows = x.shape[1] // GRID_W
    rope = _axial_rope_tables(n_rows)
    silu_c = jax.nn.silu(c)[:, None, :]
    silu_cc = jax.nn.silu(c_ctx)
    h_lat, h_ctx = x, ctx
    for i in range(DEPTH):
        need_ctx = i < DEPTH - 1
        mod_lat = jnp.split(silu_c @ ada_w[i] + ada_b[i], 6, axis=-1)
        mod_ctx = jnp.split(silu_cc @ ada_w[i] + ada_b[i], 6, axis=-1)
        a_lat = _modulate(_rmsnorm(h_lat, norm_w[i, 0]), mod_lat[0], mod_lat[1])
        a_ctx = _modulate(_rmsnorm(h_ctx, norm_w[i, 0]), mod_ctx[0], mod_ctx[1])
        j = i // 2
        if i % 2 == 0:
            m_lat, m_ctx = _even_mixer(a_lat, a_ctx, ev_w_in[j], ev_w_out[j], ev_lambda[j], ev_subln_w[j],
                                       ev_conv_w[j], ev_conv_b[j], ev_ln_w[j], ev_ln_b[j], rope, i, need_ctx)
        else:
            m_lat, m_ctx = _odd_mixer(a_lat, a_ctx, od_w_in[j], od_w_out[j], s5_lam_re[j], s5_lam_im[j],
                                      s5_log_step[j], s5_b_re[j], s5_b_im[j], s5_c_re[j], s5_c_im[j], s5_d[j],
                                      s5_glu_w[j], s5_glu_b[j], m2_conv_w[j], m2_conv_b[j], m2_a_log[j],
                                      m2_dt_bias[j], m2_d[j], m2_norm_w[j], need_ctx)
        h_lat = h_lat + mod_lat[2] * m_lat
        h_lat = h_lat + mod_lat[5] * _swiglu(_modulate(_rmsnorm(h_lat, norm_w[i, 1]), mod_lat[3], mod_lat[4]),
                                             ffn_w1[i], ffn_w3[i], ffn_w2[i])
        if need_ctx:
            h_ctx = h_ctx + mod_ctx[2] * m_ctx
            h_ctx = h_ctx + mod_ctx[5] * _swiglu(_modulate(_rmsnorm(h_ctx, norm_w[i, 1]), mod_ctx[3], mod_ctx[4]),
                                                 ffn_w1[i], ffn_w3[i], ffn_w2[i])
    return _rmsnorm(h_lat, final_norm_w)
```

```python
import functools
import math

import jax
import jax.numpy as jnp
from jax import lax
from jax.experimental import pallas as pl
from jax.experimental.pallas import tpu as pltpu

F32 = jnp.float32
BF16 = jnp.bfloat16

D_MODEL = 4096
SEQ = 8192
DEPTH = 2
CTX_LEN = 256
GRID_W = 64
RMS_EPS = 1e-6
LN_EPS = 1e-5
A_WIDTH = D_MODEL // 2
A_HEAD_DIM = 64
A_HEADS = A_WIDTH // (2 * A_HEAD_DIM)
B_WIDTH = D_MODEL - A_WIDTH
CONV_WIDTH = 31
ROPE_BASE = 10000.0
ROPE_AXIS_DIM = A_HEAD_DIM // 2
EV_V0 = A_WIDTH
EV_Q0 = 2 * A_WIDTH
EV_B0 = 3 * A_WIDTH
S5_WIDTH = D_MODEL // 4
S5_GROUP = 16
S5_GROUPS = S5_WIDTH // S5_GROUP
S5_STATE = 64
M2_INNER = D_MODEL - S5_WIDTH
M2_HEAD_DIM = 64
M2_HEADS = M2_INNER // M2_HEAD_DIM
M2_GROUPS = 8
M2_STATE = 128
M2_CONV = 5
M2_CONV_DIM = M2_INNER + 2 * M2_GROUPS * M2_STATE
OD_DT0 = S5_WIDTH + M2_CONV_DIM
OD_Z0 = OD_DT0 + 2 * M2_HEADS

V7X_VMEM_BYTES = 64 * 1024 * 1024
LANES = 128
SUBLANES = 8
HALO = 16
S5_CHUNK = 16
SSD_CHUNK = 128
NEG_BIG = -1e30


def _cparams(sem, vmem_bytes):
    return pltpu.CompilerParams(dimension_semantics=sem,
                                vmem_limit_bytes=int(min(vmem_bytes, V7X_VMEM_BYTES - (4 << 20))))


def _pick(n, cands):
    for c in cands:
        if n % c == 0:
            return c
    raise ValueError(f"no block size in {cands} divides {n}")


def _nbytes(shape, dtype):
    return math.prod(shape) * jnp.dtype(dtype).itemsize


def _vmem(blocks, scratch=0):
    return 2 * sum(_nbytes(s, d) for s, d in blocks) + scratch + (12 << 20)


def _sigmoid(x):
    return jax.nn.sigmoid(x)


def _split3(x):
    hi = x.astype(BF16)
    r1 = x - hi.astype(F32)
    mid = r1.astype(BF16)
    lo = (r1 - mid.astype(F32)).astype(BF16)
    return hi, mid, lo


def _row_is_lat(i, bm, n_lat):
    row = i * bm + lax.broadcasted_iota(jnp.int32, (bm, 1), 0)
    return row < n_lat


def _ada_kernel(x_ref, w_ref, b_ref, o_ref):
    x = x_ref[...]
    xs = (x * _sigmoid(x)).astype(BF16)
    o_ref[0] = jnp.dot(xs, w_ref[0].astype(BF16), preferred_element_type=F32) + b_ref[0]


def _ada_mods(c, c_ctx, ada_w, ada_b):
    depth, d, n6 = ada_w.shape
    xin = jnp.zeros((SUBLANES, d), F32).at[0].set(c[0]).at[1].set(c_ctx)
    bn = _pick(n6, (512, 256, 128))
    return pl.pallas_call(
        _ada_kernel,
        grid=(depth, n6 // bn),
        in_specs=[pl.BlockSpec((SUBLANES, d), lambda l, j: (0, 0)),
                  pl.BlockSpec((1, d, bn), lambda l, j: (l, 0, j)),
                  pl.BlockSpec((1, 1, bn), lambda l, j: (l, 0, j))],
        out_specs=pl.BlockSpec((1, SUBLANES, bn), lambda l, j: (l, 0, j)),
        out_shape=jax.ShapeDtypeStruct((depth, SUBLANES, n6), F32),
        compiler_params=_cparams(("parallel", "parallel"), _vmem([((d, bn), F32), ((d, bn), BF16)])),
        name="ada_mods",
    )(xin, ada_w, ada_b.reshape(depth, 1, n6))


def _normmod_kernel(h_ref, nw_ref, sh_ref, sc_ref, o_ref, *, bm, n_lat):
    x = h_ref[...]
    y = x * lax.rsqrt(jnp.mean(x * x, axis=-1, keepdims=True) + RMS_EPS) * nw_ref[...]
    is_lat = _row_is_lat(pl.program_id(0), bm, n_lat)
    sh = jnp.where(is_lat, sh_ref[0:1, :], sh_ref[1:2, :])
    sc = jnp.where(is_lat, sc_ref[0:1, :], sc_ref[1:2, :])
    o_ref[...] = (y * (1.0 + sc) + sh).astype(BF16)


def _normmod(h, nw, mods, k_shift, k_scale, n_lat):
    t, d = h.shape
    bm = _pick(t, (256, 128))
    return pl.pallas_call(
        functools.partial(_normmod_kernel, bm=bm, n_lat=n_lat),
        grid=(t // bm,),
        in_specs=[pl.BlockSpec((bm, d), lambda i: (i, 0)),
                  pl.BlockSpec((1, d), lambda i: (0, 0)),
                  pl.BlockSpec((SUBLANES, d), lambda i: (0, k_shift)),
                  pl.BlockSpec((SUBLANES, d), lambda i: (0, k_scale))],
        out_specs=pl.BlockSpec((bm, d), lambda i: (i, 0)),
        out_shape=jax.ShapeDtypeStruct((t, d), BF16),
        compiler_params=_cparams(("parallel",), _vmem([((bm, d), F32), ((bm, d), BF16), ((bm, d), F32)])),
        name="norm_modulate",
    )(h, nw.reshape(1, d), mods, mods)


def _final_norm_kernel(h_ref, nw_ref, o_ref):
    x = h_ref[...]
    o_ref[...] = x * lax.rsqrt(jnp.mean(x * x, axis=-1, keepdims=True) + RMS_EPS) * nw_ref[...]


def _final_norm(h, nw, n_lat):
    t, d = h.shape
    bm = _pick(n_lat, (256, 128))
    return pl.pallas_call(
        _final_norm_kernel,
        grid=(n_lat // bm,),
        in_specs=[pl.BlockSpec((bm, d), lambda i: (i, 0)),
                  pl.BlockSpec((1, d), lambda i: (0, 0))],
        out_specs=pl.BlockSpec((bm, d), lambda i: (i, 0)),
        out_shape=jax.ShapeDtypeStruct((n_lat, d), F32),
        compiler_params=_cparams(("parallel",), _vmem([((bm, d), F32), ((bm, d), F32), ((bm, d), F32)])),
        name="final_norm",
    )(h, nw.reshape(1, d))


def _mm_plain_kernel(x_ref, w_ref, o_ref):
    o_ref[...] = jnp.dot(x_ref[...], w_ref[...], preferred_element_type=F32).astype(o_ref.dtype)


def _mm_plain(x, w, out_dtype, *, col_off=0, n_cols=None, bm=None, bn=None):
    t, k = x.shape
    n_cols = w.shape[1] - col_off if n_cols is None else n_cols
    bm = bm or _pick(t, (768, 512, 256, 128))
    bn = bn or _pick(math.gcd(n_cols, col_off) if col_off else n_cols, (512, 256, 128))
    ob = col_off // bn
    return pl.pallas_call(
        _mm_plain_kernel,
        grid=(t // bm, n_cols // bn),
        in_specs=[pl.BlockSpec((bm, k), lambda i, j: (i, 0)),
                  pl.BlockSpec((k, bn), lambda i, j: (0, j + ob))],
        out_specs=pl.BlockSpec((bm, bn), lambda i, j: (i, j)),
        out_shape=jax.ShapeDtypeStruct((t, n_cols), out_dtype),
        compiler_params=_cparams(("parallel", "arbitrary"),
                                 _vmem([((bm, k), BF16), ((k, bn), BF16), ((bm, bn), F32)])),
        name="matmul",
    )(x, w)


def _qkv_kernel(x_ref, w_ref, cos_ref, s1_ref, s2_ref, o_ref, *, nb_region, bn, q_scale):
    acc = jnp.dot(x_ref[...], w_ref[...], preferred_element_type=F32)
    region = pl.program_id(1) // nb_region

    @pl.when(region == 1)
    def _():
        o_ref[...] = acc.astype(BF16)

    @pl.when(region != 1)
    def _():
        reps = bn // LANES
        cos = jnp.tile(cos_ref[...], (1, reps))
        s1 = jnp.tile(s1_ref[...], (1, reps))
        s2 = jnp.tile(s2_ref[...], (1, reps))
        out = acc * cos + pltpu.roll(acc, 16, axis=1) * s1 + pltpu.roll(acc, bn - 16, axis=1) * s2
        scale = jnp.where(region == 2, q_scale, 1.0).astype(F32)
        o_ref[...] = (out * scale).astype(BF16)


def _qkv_proj(a, w_in, cos_t, s1_t, s2_t):
    t, k = a.shape
    bm = _pick(t, (768, 512, 256, 128))
    bn = _pick(A_WIDTH, (512, 256, 128))
    n_cols = 3 * A_WIDTH
    return pl.pallas_call(
        functools.partial(_qkv_kernel, nb_region=A_WIDTH // bn, bn=bn, q_scale=A_HEAD_DIM ** -0.5),
        grid=(t // bm, n_cols // bn),
        in_specs=[pl.BlockSpec((bm, k), lambda i, j: (i, 0)),
                  pl.BlockSpec((k, bn), lambda i, j: (0, j)),
                  pl.BlockSpec((bm, LANES), lambda i, j: (i, 0)),
                  pl.BlockSpec((bm, LANES), lambda i, j: (i, 0)),
                  pl.BlockSpec((bm, LANES), lambda i, j: (i, 0))],
        out_specs=pl.BlockSpec((bm, bn), lambda i, j: (i, j)),
        out_shape=jax.ShapeDtypeStruct((t, n_cols), BF16),
        compiler_params=_cparams(("parallel", "arbitrary"),
                                 _vmem([((bm, k), BF16), ((k, bn), BF16), ((bm, bn), F32), ((bm, bn), F32)])),
        name="qkv_rope_proj",
    )(a, w_in, cos_t, s1_t, s2_t)


def _glu_kernel(x_ref, wa_ref, wg_ref, o_ref):
    x = x_ref[...]
    a = jnp.dot(x, wa_ref[...], preferred_element_type=F32)
    g = jnp.dot(x, wg_ref[...], preferred_element_type=F32)
    o_ref[...] = (a * _sigmoid(g)).astype(o_ref.dtype)


def _swiglu_up_kernel(x_ref, w1_ref, w3_ref, o_ref):
    x = x_ref[...]
    a = jnp.dot(x, w1_ref[...], preferred_element_type=F32)
    b = jnp.dot(x, w3_ref[...], preferred_element_type=F32)
    o_ref[...] = (a * _sigmoid(a) * b).astype(o_ref.dtype)


def _mm_pair(kern, x, wa, wg, out_dtype, *, off_a=0, off_g=0, n_cols=None, name="matmul_pair"):
    t, k = x.shape
    n_cols = wa.shape[1] if n_cols is None else n_cols
    bm = _pick(t, (768, 512, 256, 128))
    g = n_cols
    for o in (off_a, off_g):
        g = math.gcd(g, o) if o else g
    bn = _pick(g, (512, 256, 128))
    oa, og = off_a // bn, off_g // bn
    return pl.pallas_call(
        kern,
        grid=(t // bm, n_cols // bn),
        in_specs=[pl.BlockSpec((bm, k), lambda i, j: (i, 0)),
                  pl.BlockSpec((k, bn), lambda i, j: (0, j + oa)),
                  pl.BlockSpec((k, bn), lambda i, j: (0, j + og))],
        out_specs=pl.BlockSpec((bm, bn), lambda i, j: (i, j)),
        out_shape=jax.ShapeDtypeStruct((t, n_cols), out_dtype),
        compiler_params=_cparams(("parallel", "arbitrary"),
                                 _vmem([((bm, k), BF16), ((k, bn), BF16), ((k, bn), BF16), ((bm, bn), F32),
                                        ((bm, bn), F32)])),
        name=name,
    )(x, wa, wg)


def _resid1_kernel(x_ref, w_ref, h_ref, gate_ref, o_ref, *, bm, n_lat):
    acc = jnp.dot(x_ref[...], w_ref[...], preferred_element_type=F32)
    is_lat = _row_is_lat(pl.program_id(0), bm, n_lat)
    gate = jnp.where(is_lat, gate_ref[0:1, :], gate_ref[1:2, :])
    o_ref[...] = h_ref[...] + gate * acc


def _resid2_kernel(x1_ref, w1_ref, x2_ref, w2_ref, h_ref, gate_ref, o_ref, *, bm, n_lat):
    acc = jnp.dot(x1_ref[...], w1_ref[...], preferred_element_type=F32)
    acc = acc + jnp.dot(x2_ref[...], w2_ref[...], preferred_element_type=F32)
    is_lat = _row_is_lat(pl.program_id(0), bm, n_lat)
    gate = jnp.where(is_lat, gate_ref[0:1, :], gate_ref[1:2, :])
    o_ref[...] = h_ref[...] + gate * acc


def _mm_resid(xs, ws, h, mods, k_gate, n_lat, *, bn_cands=(512, 256, 128), name="matmul_resid"):
    t, d = h.shape
    bm = _pick(t, (768, 512, 256, 128))
    bn = _pick(d, bn_cands)
    nbd = d // bn
    in_specs, args, blocks = [], [], []
    for x, w in zip(xs, ws):
        k = x.shape[1]
        in_specs += [pl.BlockSpec((bm, k), lambda i, j: (i, 0)), pl.BlockSpec((k, bn), lambda i, j: (0, j))]
        args += [x, w]
        blocks += [((bm, k), BF16), ((k, bn), BF16)]
    in_specs += [pl.BlockSpec((bm, bn), lambda i, j: (i, j)),
                 pl.BlockSpec((SUBLANES, bn), lambda i, j: (0, k_gate * nbd + j))]
    args += [h, mods]
    blocks += [((bm, bn), F32), ((bm, bn), F32), ((bm, bn), F32)]
    kern = _resid1_kernel if len(xs) == 1 else _resid2_kernel
    return pl.pallas_call(
        functools.partial(kern, bm=bm, n_lat=n_lat),
        grid=(t // bm, d // bn),
        in_specs=in_specs,
        out_specs=pl.BlockSpec((bm, bn), lambda i, j: (i, j)),
        out_shape=jax.ShapeDtypeStruct((t, d), F32),
        compiler_params=_cparams(("parallel", "arbitrary"), _vmem(blocks)),
        name=name,
    )(*args)


def _attn_kernel(lam_ref, subw_ref, q_ref, k_ref, v_ref, o_ref, *, bq, bk, n_lat, n_ctx, lam_init):
    qi = pl.program_id(1)
    q = q_ref[...]
    lane = lax.broadcasted_iota(jnp.int32, (bq, LANES), 1)
    zero = jnp.zeros_like(q)
    qm = jnp.concatenate([jnp.where(lane < A_HEAD_DIM, q, zero),
                          jnp.where(lane >= A_HEAD_DIM, q, zero)], axis=0)

    def step(kc, vc, carry):
        m, l, acc = carry
        s = lax.dot_general(qm, kc, (((1,), (1,)), ((), ())), preferred_element_type=F32)
        m_new = jnp.maximum(m, jnp.max(s, axis=1, keepdims=True))
        alpha = jnp.exp(m - m_new)
        p = jnp.exp(s - m_new)
        l = alpha * l + jnp.sum(p, axis=1, keepdims=True)
        acc = alpha * acc + jnp.dot(p.astype(BF16), vc, preferred_element_type=F32)
        return m_new, l, acc

    init = (jnp.full((2 * bq, 1), NEG_BIG, F32), jnp.zeros((2 * bq, 1), F32),
            jnp.zeros((2 * bq, LANES), F32))
    carry = step(k_ref[n_lat:n_lat + n_ctx, :], v_ref[n_lat:n_lat + n_ctx, :], init)
    n_steps = jnp.where(qi < n_lat // bq, n_lat // bk, 0)

    def body(c, carry):
        start = pl.multiple_of(c * bk, bk)
        return step(k_ref[pl.ds(start, bk), :], v_ref[pl.ds(start, bk), :], carry)

    m, l, acc = lax.fori_loop(0, n_steps, body, carry)
    lam = lam_ref[...]
    lam_full = (jnp.exp(jnp.sum(lam[0:1] * lam[1:2], axis=1, keepdims=True))
                - jnp.exp(jnp.sum(lam[2:3] * lam[3:4], axis=1, keepdims=True)) + lam_init)
    o = acc[:bq] / l[:bq] - lam_full * (acc[bq:] / l[bq:])
    y = o * lax.rsqrt(jnp.mean(o * o, axis=-1, keepdims=True) + 1e-5) * subw_ref[...]
    o_ref[...] = (y * (1.0 - lam_init)).astype(BF16)


def _attention(kvq, lam, subw, n_lat, n_ctx, lam_init):
    t = kvq.shape[0]
    bq = _pick(math.gcd(n_lat, n_ctx), (256, 128))
    bk = _pick(n_lat, (512, 256, 128))
    nh = A_HEADS
    return pl.pallas_call(
        functools.partial(_attn_kernel, bq=bq, bk=bk, n_lat=n_lat, n_ctx=n_ctx, lam_init=lam_init),
        grid=(nh, t // bq),
        in_specs=[pl.BlockSpec((4, A_HEAD_DIM), lambda h, i: (0, 0)),
                  pl.BlockSpec((1, LANES), lambda h, i: (0, 0)),
                  pl.BlockSpec((bq, LANES), lambda h, i: (i, 2 * nh + h)),
                  pl.BlockSpec((t, LANES), lambda h, i: (0, h)),
                  pl.BlockSpec((t, LANES), lambda h, i: (0, nh + h))],
        out_specs=pl.BlockSpec((bq, LANES), lambda h, i: (i, h)),
        out_shape=jax.ShapeDtypeStruct((t, A_WIDTH), BF16),
        compiler_params=_cparams(("parallel", "arbitrary"),
                                 _vmem([((t, LANES), BF16), ((t, LANES), BF16)], scratch=16 << 20)),
        name="diff_attention",
    )(lam, subw.reshape(1, LANES), kvq, kvq, kvq)


def _dwconv_kernel(prev_ref, cur_ref, next_ref, w_ref, b_ref, lnw_ref, lnb_ref, o_ref, xx_ref, y_ref,
                   *, bm, cb, taps, nb_lat, nb_all, layernorm):
    i = pl.program_id(0)
    has_prev = jnp.logical_and(i != 0, i != nb_lat)
    has_next = jnp.logical_and(i != nb_lat - 1, i != nb_all - 1)
    xx_ref[0:HALO, :] = jnp.where(has_prev, prev_ref[...], 0.0)
    xx_ref[HALO:HALO + bm, :] = cur_ref[...]
    xx_ref[HALO + bm:2 * HALO + bm, :] = jnp.where(has_next, next_ref[...], 0.0)
    pad = (taps - 1) // 2
    rs = min(bm, 64)
    cs = min(cb, 512)
    for c0 in range(0, cb, cs):
        wts = [w_ref[k:k + 1, c0:c0 + cs] for k in range(taps)]
        bias = b_ref[:, c0:c0 + cs]
        for r0 in range(0, bm, rs):
            acc = jnp.broadcast_to(bias, (rs, cs))
            for k in range(taps):
                lo = r0 + HALO + k - pad
                acc = acc + xx_ref[lo:lo + rs, c0:c0 + cs] * wts[k]
            y_ref[r0:r0 + rs, c0:c0 + cs] = acc
    y = y_ref[...]
    if layernorm:
        mu = jnp.mean(y, axis=-1, keepdims=True)
        var = jnp.mean(jnp.square(y - mu), axis=-1, keepdims=True)
        y = (y - mu) * lax.rsqrt(var + LN_EPS) * lnw_ref[...] + lnb_ref[...]
    o_ref[...] = (y * _sigmoid(y)).astype(o_ref.dtype)


def _dwconv_silu(x, w, b, n_lat, out_dtype, *, col_off=0, n_cols=None, ln=None):
    t = x.shape[0]
    taps = w.shape[0]
    n_cols = x.shape[1] - col_off if n_cols is None else n_cols
    bm = _pick(math.gcd(n_lat, t - n_lat), (256, 128))
    cb = n_cols if ln is not None else _pick(math.gcd(n_cols, col_off) if col_off else n_cols, (1024, 512, 256, 128))
    ob = col_off // cb
    hb = bm // HALO
    n_halo_blocks = t // HALO
    lnw, lnb = ln if ln is not None else (jnp.ones((n_cols,), F32), jnp.zeros((n_cols,), F32))
    return pl.pallas_call(
        functools.partial(_dwconv_kernel, bm=bm, cb=cb, taps=taps, nb_lat=n_lat // bm, nb_all=t // bm,
                          layernorm=ln is not None),
        grid=(t // bm, n_cols // cb),
        in_specs=[pl.BlockSpec((HALO, cb), lambda i, j: (jnp.maximum(i * hb - 1, 0), j + ob)),
                  pl.BlockSpec((bm, cb), lambda i, j: (i, j + ob)),
                  pl.BlockSpec((HALO, cb), lambda i, j: (jnp.minimum((i + 1) * hb, n_halo_blocks - 1), j + ob)),
                  pl.BlockSpec((taps, cb), lambda i, j: (0, j)),
                  pl.BlockSpec((1, cb), lambda i, j: (0, j)),
                  pl.BlockSpec((1, cb), lambda i, j: (0, j)),
                  pl.BlockSpec((1, cb), lambda i, j: (0, j))],
        out_specs=pl.BlockSpec((bm, cb), lambda i, j: (i, j)),
        out_shape=jax.ShapeDtypeStruct((t, n_cols), out_dtype),
        scratch_shapes=[pltpu.VMEM((bm + 2 * HALO, cb), F32), pltpu.VMEM((bm, cb), F32)],
        compiler_params=_cparams(("parallel", "parallel"),
                                 _vmem([((bm, cb), F32), ((bm, cb), F32)], scratch=3 * bm * cb * 4)),
        name="dwconv_silu",
    )(x, x, x, w, b.reshape(1, n_cols), lnw.reshape(1, n_cols), lnb.reshape(1, n_cols))


def _s5_build_kernel(u_ref, lam_ref, step_ref, bt_ref, c_ref, yz_ref, et_ref, adv_ref):
    L, H, P = S5_CHUNK, S5_GROUP, S5_STATE
    lane = lax.broadcasted_iota(jnp.int32, (1, 2 * P), 1)
    first = lane < P
    sgn = jnp.where(first, -1.0, 1.0).astype(F32)
    conj_sgn = -sgn

    def swap(x):
        return pltpu.roll(x, P, axis=1)

    def dupr(x):
        return jnp.where(first, x, swap(x))

    def dupi(x):
        return jnp.where(first, swap(x), x)

    def cmul(a, b):
        return dupr(a) * b + (sgn * dupi(a)) * swap(b)

    one = jnp.where(first, 1.0, 0.0).astype(F32)
    mats, ets, advs = [], [], []
    for d in range(2):
        lam = lam_ref[0, d:d + 1, :]
        z = lam * jnp.exp(step_ref[0, d:d + 1, :])
        ang = dupi(z)
        lbar = jnp.exp(dupr(z)) * jnp.where(first, jnp.cos(ang), jnp.sin(ang))
        den = dupr(lam) * dupr(lam) + dupi(lam) * dupi(lam)
        qcoef = cmul(lbar - one, lam * conj_sgn) / den
        bbar = cmul(qcoef, bt_ref[0])
        cmat = c_ref[0, d * H:(d + 1) * H, :]
        pw = [one]
        for _ in range(L):
            pw.append(cmul(pw[-1], lbar))
        cpow = [cmul(pw[t], cmat) * conj_sgn for t in range(L + 1)]
        if d == 0:
            kern_rows = jnp.concatenate([cpow[t] for t in range(L)], axis=0)
            inj = jnp.concatenate([cmul(pw[L - 1 - j], bbar) for j in range(L)], axis=0)
            read = jnp.concatenate([cpow[t + 1] for t in range(L)], axis=0)
        else:
            kern_rows = jnp.concatenate([cpow[L - 1 - t] for t in range(L)], axis=0)
            inj = jnp.concatenate([cmul(pw[j], bbar) for j in range(L)], axis=0)
            read = jnp.concatenate([cpow[L - t] for t in range(L)], axis=0)
        base = lax.dot_general(bbar.astype(BF16), kern_rows.astype(BF16), (((1,), (1,)), ((), ())),
                               preferred_element_type=F32)
        lane_o = lax.broadcasted_iota(jnp.int32, (H, L * H), 1)
        blocks = []
        for j in range(L):
            if d == 0:
                sh = (j * H) % (L * H)
                blk = jnp.where(lane_o >= j * H, pltpu.roll(base, sh, axis=1) if sh else base, 0.0)
            else:
                sh = (L * H - (L - 1 - j) * H) % (L * H)
                blk = jnp.where(lane_o < (j + 1) * H, pltpu.roll(base, sh, axis=1) if sh else base, 0.0)
            blocks.append(blk)
        mats.append((jnp.concatenate(blocks, axis=0), inj))
        ets.append(read)
        pl_ = pw[L]
        advs.append((dupr(pl_), sgn * dupi(pl_)))
    mix = (mats[0][0] + mats[1][0]).astype(BF16)
    rhs = jnp.concatenate([mix, mats[0][1].astype(BF16), mats[1][1].astype(BF16)], axis=1)
    yz_ref[0] = jnp.dot(u_ref[0], rhs, preferred_element_type=F32)
    et_ref[0] = jnp.concatenate([ets[0], ets[1]], axis=1).astype(BF16)
    adv_ref[0] = jnp.concatenate([jnp.concatenate([advs[0][0], advs[1][0]], axis=1),
                                  jnp.concatenate([advs[0][1], advs[1][1]], axis=1)], axis=0)


def _s5_scan_kernel(z_ref, adv_ref, s_ref, *, n_chunks, nc_ctx):
    P = S5_STATE
    a1 = adv_ref[:, 0, :]
    a2 = adv_ref[:, 1, :]
    gb = a1.shape[0]
    nl = n_chunks - nc_ctx

    def swap_halves(x):
        return jnp.concatenate([pltpu.roll(x[:, :2 * P], P, axis=1), pltpu.roll(x[:, 2 * P:], P, axis=1)], axis=1)

    def body(i, s):
        cf = jnp.where(i < nc_ctx, nl + i, i - nc_ctx)
        cb = n_chunks - 1 - i
        zf = z_ref[cf][:, :2 * P]
        zb = z_ref[cb][:, 2 * P:]
        s_ref[cf, :, 0:2 * P] = s[:, :2 * P].astype(s_ref.dtype)
        s_ref[cb, :, 2 * P:4 * P] = s[:, 2 * P:].astype(s_ref.dtype)
        return a1 * s + a2 * swap_halves(s) + jnp.concatenate([zf, zb], axis=1)

    lax.fori_loop(0, n_chunks, body, jnp.zeros((gb, 4 * P), F32))


def _s5_out_kernel(yz_ref, s_ref, et_ref, o_ref):
    n = o_ref.shape[2]
    o_ref[0] = yz_ref[0, :, 0:n] + lax.dot_general(s_ref[0], et_ref[0], (((1,), (1,)), ((), ())),
                                                  preferred_element_type=F32)


def _s5_finish_kernel(y_ref, u_ref, d_ref, w_ref, b_ref, o_ref):
    y = y_ref[...] + d_ref[...] * u_ref[...]
    g = 0.5 * y * (1.0 + jnp.tanh(math.sqrt(2.0 / math.pi) * (y + 0.044715 * (y * y * y))))
    gate = jnp.dot(g.astype(BF16), w_ref[...], preferred_element_type=F32) + b_ref[...]
    o_ref[...] = (g * _sigmoid(gate)).astype(BF16)


def _s5_branch(u, lam_re, lam_im, log_step, b_re, b_im, c_re, c_im, d_skip, glu_w, glu_b, n_lat):
    t = u.shape[0]
    G, H, P, L = S5_GROUPS, S5_GROUP, S5_STATE, S5_CHUNK
    nch = t // L
    nc_ctx = (t - n_lat) // L
    lam_p = jnp.concatenate([lam_re, lam_im], axis=-1).transpose(1, 0, 2)
    step_p = jnp.broadcast_to(log_step.T[:, :, None], (G, 2, 2 * P))
    bt_p = jnp.concatenate([b_re, b_im], axis=1).transpose(0, 2, 1)
    c_p = jnp.concatenate([c_re, c_im], axis=-1).transpose(1, 0, 2, 3).reshape(G, 2 * H, 2 * P)
    ug = u.astype(BF16).reshape(nch, L, G, H).transpose(2, 0, 1, 3).reshape(G, nch, L * H)
    LH = L * H
    yz, et, adv = pl.pallas_call(
        _s5_build_kernel,
        grid=(G,),
        in_specs=[pl.BlockSpec((1, nch, LH), lambda g: (g, 0, 0)),
                  pl.BlockSpec((1, 2, 2 * P), lambda g: (g, 0, 0)),
                  pl.BlockSpec((1, 2, 2 * P), lambda g: (g, 0, 0)),
                  pl.BlockSpec((1, H, 2 * P), lambda g: (g, 0, 0)),
                  pl.BlockSpec((1, 2 * H, 2 * P), lambda g: (g, 0, 0))],
        out_specs=[pl.BlockSpec((1, nch, LH + 4 * P), lambda g: (g, 0, 0)),
                   pl.BlockSpec((1, LH, 4 * P), lambda g: (g, 0, 0)),
                   pl.BlockSpec((1, 2, 4 * P), lambda g: (g, 0, 0))],
        out_shape=[jax.ShapeDtypeStruct((G, nch, LH + 4 * P), F32),
                   jax.ShapeDtypeStruct((G, LH, 4 * P), BF16),
                   jax.ShapeDtypeStruct((G, 2, 4 * P), F32)],
        compiler_params=_cparams(("parallel",), _vmem([((nch, LH), BF16), ((nch, LH + 4 * P), F32)])),
        name="s5_build_apply",
    )(ug, lam_p, step_p, bt_p, c_p)
    z_t = yz[:, :, LH:].transpose(1, 0, 2)
    gb = _pick(G, (16, 8))
    s_in = pl.pallas_call(
        functools.partial(_s5_scan_kernel, n_chunks=nch, nc_ctx=nc_ctx),
        grid=(G // gb,),
        in_specs=[pl.BlockSpec((nch, gb, 4 * P), lambda g: (0, g, 0)),
                  pl.BlockSpec((gb, 2, 4 * P), lambda g: (g, 0, 0))],
        out_specs=pl.BlockSpec((nch, gb, 4 * P), lambda g: (0, g, 0)),
        out_shape=jax.ShapeDtypeStruct((nch, G, 4 * P), BF16),
        compiler_params=_cparams(("parallel",), _vmem([((nch, gb, 4 * P), F32), ((nch, gb, 4 * P), BF16)])),
        name="s5_chunk_scan",
    )(z_t, adv)
    s_g = s_in.transpose(1, 0, 2)
    y = pl.pallas_call(
        _s5_out_kernel,
        grid=(G,),
        in_specs=[pl.BlockSpec((1, nch, LH + 4 * P), lambda g: (g, 0, 0)),
                  pl.BlockSpec((1, nch, 4 * P), lambda g: (g, 0, 0)),
                  pl.BlockSpec((1, LH, 4 * P), lambda g: (g, 0, 0))],
        out_specs=pl.BlockSpec((1, nch, LH), lambda g: (g, 0, 0)),
        out_shape=jax.ShapeDtypeStruct((G, nch, LH), F32),
        compiler_params=_cparams(("parallel",), _vmem([((nch, LH + 4 * P), F32), ((nch, 4 * P), BF16),
                                                       ((nch, LH), F32)])),
        name="s5_readout",
    )(yz, s_g, et)
    y_tok = y.reshape(G, nch, L, H).transpose(1, 2, 0, 3).reshape(t, G * H)
    w = G * H
    bm = _pick(t, (256, 128))
    return pl.pallas_call(
        _s5_finish_kernel,
        grid=(t // bm,),
        in_specs=[pl.BlockSpec((bm, w), lambda i: (i, 0)),
                  pl.BlockSpec((bm, w), lambda i: (i, 0)),
                  pl.BlockSpec((1, w), lambda i: (0, 0)),
                  pl.BlockSpec((w, w), lambda i: (0, 0)),
                  pl.BlockSpec((1, w), lambda i: (0, 0))],
        out_specs=pl.BlockSpec((bm, w), lambda i: (i, 0)),
        out_shape=jax.ShapeDtypeStruct((t, w), BF16),
        compiler_params=_cparams(("parallel",), _vmem([((bm, w), F32), ((bm, w), F32), ((w, w), BF16),
                                                       ((bm, w), F32)])),
        name="s5_gelu_glu",
    )(y_tok, u, d_skip.reshape(1, w), glu_w.astype(BF16), glu_b.reshape(1, w))


def _ssd_kernel(xs_ref, b_ref, c_ref, dt_ref, sel_ref, bias_ref, alog_ref, y_ref, state_ref, *, hpg):
    L = SSD_CHUNK
    hd = M2_HEAD_DIM
    d = pl.program_id(0)
    is_f = d == 0

    @pl.when(pl.program_id(2) == 0)
    def _():
        state_ref[...] = jnp.zeros_like(state_ref)

    sel = sel_ref[0, 0]
    hi, mid, lo = _split3(dt_ref[...])
    dt_raw = (jnp.dot(hi, sel, preferred_element_type=F32) + jnp.dot(mid, sel, preferred_element_type=F32)
              + jnp.dot(lo, sel, preferred_element_type=F32))
    v = dt_raw + bias_ref[0, 0]
    dt = jnp.maximum(v, 0.0) + jnp.log1p(jnp.exp(-jnp.abs(v)))
    da = dt * (-jnp.exp(alog_ref[0, 0]))
    ti = lax.broadcasted_iota(jnp.int32, (L, L), 0)
    si = lax.broadcasted_iota(jnp.int32, (L, L), 1)
    tri = jnp.where(si <= ti, 1.0, 0.0).astype(BF16)
    hi, mid, lo = _split3(da)
    cum = (jnp.dot(tri, hi, preferred_element_type=F32) + jnp.dot(tri, mid, preferred_element_type=F32)
           + jnp.dot(tri, lo, preferred_element_type=F32))
    total = cum[L - 1:L, :]
    cumq = jnp.where(is_f, cum, cum - da)
    w_out = jnp.exp(jnp.where(is_f, cumq, total - cumq))
    w_state = jnp.exp(jnp.where(is_f, total - cumq, cumq))
    xsdt = xs_ref[...] * dt
    cb_ = c_ref[...]
    bb_ = b_ref[...]
    s_old = state_ref[...]
    y_off = w_out * jnp.dot(cb_, s_old.astype(BF16), preferred_element_type=F32)
    state_ref[...] = jnp.exp(total) * s_old + lax.dot_general(
        bb_, (w_state * xsdt).astype(BF16), (((0,), (0,)), ((), ())), preferred_element_type=F32)
    cbm = lax.dot_general(cb_, bb_, (((1,), (1,)), ((), ())), preferred_element_type=F32)
    cum_t = cumq.T
    mask = jnp.where(is_f, ti - si, si - ti) >= 0
    lane = lax.broadcasted_iota(jnp.int32, (L, LANES), 1)
    xb = xsdt.astype(BF16)
    zero = jnp.zeros((L, LANES), BF16)
    outs = []
    for pr in range(hpg // 2):
        gs = []
        for r in (2 * pr, 2 * pr + 1):
            col = cumq[:, hd * r:hd * r + 1]
            row = cum_t[hd * r:hd * r + 1, :]
            diff = jnp.where(is_f, col - row, row - col)
            decay = jnp.exp(jnp.where(mask, diff, NEG_BIG))
            gs.append((cbm * decay).astype(BF16))
        xp = xb[:, LANES * pr:LANES * (pr + 1)]
        rhs = jnp.concatenate([jnp.where(lane < hd, xp, zero), jnp.where(lane >= hd, xp, zero)], axis=0)
        outs.append(jnp.dot(jnp.concatenate(gs, axis=1), rhs, preferred_element_type=F32))
    y_ref[0] = jnp.concatenate(outs, axis=1) + y_off


def _ssd_finish_kernel(yf_ref, yb_ref, xs_ref, z_ref, d_ref, nw_ref, o_ref, *, n_groups):
    y = yf_ref[0] + yb_ref[0] + d_ref[...] * xs_ref[...]
    z = z_ref[...]
    y = y * (z * _sigmoid(z))
    gw = y.shape[1] // n_groups
    parts = []
    for g in range(n_groups):
        seg = y[:, g * gw:(g + 1) * gw]
        parts.append(seg * lax.rsqrt(jnp.mean(seg * seg, axis=-1, keepdims=True) + RMS_EPS))
    o_ref[...] = (jnp.concatenate(parts, axis=1) * nw_ref[...]).astype(BF16)


def _ssd_branch(xs, bm_, cm_, dt_raw, z, a_log, dt_bias, d_skip, norm_w, n_lat):
    t, inner = xs.shape
    L = SSD_CHUNK
    ng = M2_GROUPS
    hpg = M2_HEADS // ng
    w = hpg * M2_HEAD_DIM
    nch = t // L
    nc_ctx = (t - n_lat) // L
    nl = nch - nc_ctx
    src = (jnp.arange(2)[:, None, None] * M2_HEADS + jnp.arange(ng)[None, :, None] * hpg
           + jnp.arange(w)[None, None, :] // M2_HEAD_DIM)
    sel = (jnp.arange(LANES)[None, None, :, None] == src[:, :, None, :]).astype(BF16)
    expand = lambda p: jnp.repeat(p, M2_HEAD_DIM, axis=-1).reshape(2, ng, 1, w)
    bias_e = expand(dt_bias)
    alog_e = expand(a_log)

    def chunk(d, c):
        fwd = jnp.where(c < nc_ctx, nl + c, c - nc_ctx)
        return jnp.where(d == 0, fwd, nch - 1 - c)

    y2 = pl.pallas_call(
        functools.partial(_ssd_kernel, hpg=hpg),
        grid=(2, ng, nch),
        in_specs=[pl.BlockSpec((L, w), lambda d, g, c: (chunk(d, c), g)),
                  pl.BlockSpec((L, M2_STATE), lambda d, g, c: (chunk(d, c), g)),
                  pl.BlockSpec((L, M2_STATE), lambda d, g, c: (chunk(d, c), g)),
                  pl.BlockSpec((L, LANES), lambda d, g, c: (chunk(d, c), 0)),
                  pl.BlockSpec((1, 1, LANES, w), lambda d, g, c: (d, g, 0, 0)),
                  pl.BlockSpec((1, 1, 1, w), lambda d, g, c: (d, g, 0, 0)),
                  pl.BlockSpec((1, 1, 1, w), lambda d, g, c: (d, g, 0, 0))],
        out_specs=pl.BlockSpec((1, L, w), lambda d, g, c: (d, chunk(d, c), g)),
        out_shape=jax.ShapeDtypeStruct((2, t, inner), F32),
        scratch_shapes=[pltpu.VMEM((M2_STATE, w), F32)],
        compiler_params=_cparams(("parallel", "parallel", "arbitrary"), 32 << 20),
        name="ssd_chunk_scan",
    )(xs, bm_, cm_, dt_raw, sel, bias_e, alog_e)
    bm = _pick(t, (256, 128))
    d_e = jnp.repeat(d_skip, M2_HEAD_DIM).reshape(1, inner)
    return pl.pallas_call(
        functools.partial(_ssd_finish_kernel, n_groups=ng),
        grid=(t // bm,),
        in_specs=[pl.BlockSpec((1, bm, inner), lambda i: (0, i, 0)),
                  pl.BlockSpec((1, bm, inner), lambda i: (1, i, 0)),
                  pl.BlockSpec((bm, inner), lambda i: (i, 0)),
                  pl.BlockSpec((bm, inner), lambda i: (i, 0)),
                  pl.BlockSpec((1, inner), lambda i: (0, 0)),
                  pl.BlockSpec((1, inner), lambda i: (0, 0))],
        out_specs=pl.BlockSpec((bm, inner), lambda i: (i, 0)),
        out_shape=jax.ShapeDtypeStruct((t, inner), BF16),
        compiler_params=_cparams(("parallel",), _vmem([((bm, inner), F32)] * 5)),
        name="ssd_gate_norm",
    )(y2, y2, xs, z, d_e, norm_w.reshape(1, inner))


def _rope_tables(n_lat, n_ctx):
    n_rows = n_lat // GRID_W
    rows = jnp.repeat(jnp.arange(n_rows), GRID_W).astype(F32)
    cols = jnp.tile(jnp.arange(GRID_W), n_rows).astype(F32)
    inv = jnp.power(ROPE_BASE, -jnp.arange(0, ROPE_AXIS_DIM, 2, dtype=F32) / ROPE_AXIS_DIM)
    ang_r = rows[:, None] * inv
    ang_c = cols[:, None] * inv
    ang = jnp.concatenate([ang_r, ang_r, ang_c, ang_c], axis=-1)
    cos, sin = jnp.cos(ang), jnp.sin(ang)
    half = ROPE_AXIS_DIM // 2
    first = (jnp.arange(A_HEAD_DIM) % ROPE_AXIS_DIM) < half
    s1 = jnp.where(first, 0.0, sin)
    s2 = jnp.where(first, -sin, 0.0)
    pad = lambda tbl, fill: jnp.concatenate([tbl, jnp.full((n_ctx, A_HEAD_DIM), fill, F32)], axis=0)
    two = lambda tbl: jnp.concatenate([tbl, tbl], axis=-1)
    return two(pad(cos, 1.0)), two(pad(s1, 0.0)), two(pad(s2, 0.0))


def _swiglu_ffn(h, mods, nw, w1, w3, w2, n_lat):
    a = _normmod(h, nw, mods, 3, 4, n_lat)
    hid = _mm_pair(_swiglu_up_kernel, a, w1, w3, BF16, name="ffn_gate_up")
    return _mm_resid([hid], [w2], h, mods, 5, n_lat, bn_cands=(256, 128), name="ffn_down_resid")


def _even_layer(h, mods, nw, w_in, w_out, lam, subw, conv_w, conv_b, ln_w, ln_b, rope, layer_idx, n_lat):
    n_ctx = h.shape[0] - n_lat
    a = _normmod(h, nw, mods, 0, 1, n_lat)
    w_in = w_in.astype(BF16)
    kvq = _qkv_proj(a, w_in, *rope)
    glu = _mm_pair(_glu_kernel, a, w_in, w_in, F32, off_a=EV_B0, off_g=EV_B0 + B_WIDTH, n_cols=B_WIDTH,
                   name="conv_glu_proj")
    lam_init = 0.8 - 0.6 * math.exp(-0.3 * layer_idx)
    att = _attention(kvq, lam, subw, n_lat, n_ctx, lam_init)
    cv = _dwconv_silu(glu, conv_w, conv_b, n_lat, BF16, ln=(ln_w, ln_b))
    w_out = w_out.astype(BF16)
    return _mm_resid([att, cv], [w_out[:A_WIDTH], w_out[A_WIDTH:]], h, mods, 2, n_lat, name="even_out_resid")


def _odd_layer(h, mods, nw, w_in, w_out, s5p, m2p, n_lat):
    a = _normmod(h, nw, mods, 0, 1, n_lat)
    w_main = w_in[:, :OD_DT0].astype(BF16)
    u = _mm_plain(a, w_main, F32, col_off=0, n_cols=S5_WIDTH)
    xbc = _mm_plain(a, w_main, F32, col_off=S5_WIDTH, n_cols=M2_CONV_DIM)
    n_dt = 2 * M2_HEADS
    w_dt = jnp.pad(w_in[:, OD_DT0:OD_Z0], ((0, 0), (0, LANES - n_dt))).astype(BF16)
    dt_raw = _mm_plain(a, w_dt, F32)
    z = _mm_plain(a, w_in[:, OD_Z0:].astype(BF16), F32)
    s5_out = _s5_branch(u, *s5p, n_lat)
    conv_w, conv_b, a_log, dt_bias, m2_d, m2_norm_w = m2p
    gn = M2_GROUPS * M2_STATE
    xs = _dwconv_silu(xbc, conv_w[:, :M2_INNER], conv_b[:M2_INNER], n_lat, F32, col_off=0, n_cols=M2_INNER)
    bmat = _dwconv_silu(xbc, conv_w[:, M2_INNER:M2_INNER + gn], conv_b[M2_INNER:M2_INNER + gn], n_lat, BF16,
                        col_off=M2_INNER, n_cols=gn)
    cmat = _dwconv_silu(xbc, conv_w[:, M2_INNER + gn:], conv_b[M2_INNER + gn:], n_lat, BF16,
                        col_off=M2_INNER + gn, n_cols=gn)
    ssd_out = _ssd_branch(xs, bmat, cmat, dt_raw, z, a_log, dt_bias, m2_d, m2_norm_w, n_lat)
    w_out = w_out.astype(BF16)
    return _mm_resid([s5_out, ssd_out], [w_out[:S5_WIDTH], w_out[S5_WIDTH:]], h, mods, 2, n_lat,
                     name="odd_out_resid")


def kernel(x, c, ctx, c_ctx, ada_w, ada_b, norm_w, ffn_w1, ffn_w3, ffn_w2, ev_w_in, ev_w_out, ev_lambda, ev_subln_w, ev_conv_w, ev_conv_b, ev_ln_w, ev_ln_b, od_w_in, od_w_out, s5_lam_re, s5_lam_im, s5_log_step, s5_b_re, s5_b_im, s5_c_re, s5_c_im, s5_d, s5_glu_w, s5_glu_b, m2_conv_w, m2_conv_b, m2_a_log, m2_dt_bias, m2_d, m2_norm_w, final_norm_w):
    assert x.shape[0] == 1 and c.shape[0] == 1 and ctx.shape[0] == 1
    n_lat, n_ctx = x.shape[1], ctx.shape[1]
    depth = ada_w.shape[0]
    h = jnp.concatenate([x[0], ctx[0]], axis=0)
    mods_all = _ada_mods(c, c_ctx, ada_w, ada_b)
    rope = _rope_tables(n_lat, n_ctx)
    for i in range(depth):
        mods = mods_all[i]
        j = i // 2
        if i % 2 == 0:
            h = _even_layer(h, mods, norm_w[i, 0], ev_w_in[j], ev_w_out[j], ev_lambda[j], ev_subln_w[j],
                            ev_conv_w[j], ev_conv_b[j], ev_ln_w[j], ev_ln_b[j], rope, i, n_lat)
        else:
            s5p = (s5_lam_re[j], s5_lam_im[j], s5_log_step[j], s5_b_re[j], s5_b_im[j], s5_c_re[j], s5_c_im[j],
                   s5_d[j], s5_glu_w[j], s5_glu_b[j])
            m2p = (m2_conv_w[j], m2_conv_b[j], m2_a_log[j], m2_dt_bias[j], m2_d[j], m2_norm_w[j])
            h = _odd_layer(h, mods, norm_w[i, 0], od_w_in[j], od_w_out[j], s5p, m2p, n_lat)
        h = _swiglu_ffn(h, mods, norm_w[i, 1], ffn_w1[i].astype(BF16), ffn_w3[i].astype(BF16),
                        ffn_w2[i].astype(BF16), n_lat)
    return _final_norm(h, final_norm_w, n_lat)[None]
```

```python
import functools
import math

import jax
import jax.numpy as jnp
from jax import lax
from jax.experimental import pallas as pl
from jax.experimental.pallas import tpu as pltpu

F32 = jnp.float32
BF16 = jnp.bfloat16

D_MODEL = 4096
SEQ = 8192
DEPTH = 2
CTX_LEN = 256
GRID_W = 64
RMS_EPS = 1e-6
LN_EPS = 1e-5
A_WIDTH = D_MODEL // 2
A_HEAD_DIM = 64
A_HEADS = A_WIDTH // (2 * A_HEAD_DIM)
B_WIDTH = D_MODEL - A_WIDTH
CONV_WIDTH = 31
ROPE_BASE = 10000.0
ROPE_AXIS_DIM = A_HEAD_DIM // 2
EV_V0 = A_WIDTH
EV_Q0 = 2 * A_WIDTH
EV_B0 = 3 * A_WIDTH
S5_WIDTH = D_MODEL // 4
S5_GROUP = 16
S5_GROUPS = S5_WIDTH // S5_GROUP
S5_STATE = 64
M2_INNER = D_MODEL - S5_WIDTH
M2_HEAD_DIM = 64
M2_HEADS = M2_INNER // M2_HEAD_DIM
M2_GROUPS = 8
M2_STATE = 128
M2_CONV = 5
M2_CONV_DIM = M2_INNER + 2 * M2_GROUPS * M2_STATE
OD_DT0 = S5_WIDTH + M2_CONV_DIM
OD_Z0 = OD_DT0 + 2 * M2_HEADS

V7X_VMEM_BYTES = 64 * 1024 * 1024
LANES = 128
SUBLANES = 8
HALO = 16
S5_CHUNK = 16
SSD_CHUNK = 128
NEG_BIG = -1e30


def _cparams(sem, vmem_bytes):
    return pltpu.CompilerParams(dimension_semantics=sem,
                                vmem_limit_bytes=int(min(vmem_bytes, V7X_VMEM_BYTES - (4 << 20))))


def _pick(n, cands):
    for c in cands:
        if n % c == 0:
            return c
    raise ValueError(f"no block size in {cands} divides {n}")


def _nbytes(shape, dtype):
    return math.prod(shape) * jnp.dtype(dtype).itemsize


def _vmem(blocks, scratch=0):
    return 2 * sum(_nbytes(s, d) for s, d in blocks) + scratch + (12 << 20)


def _sigmoid(x):
    return jax.nn.sigmoid(x)


def _split3(x):
    hi = x.astype(BF16)
    r1 = x - hi.astype(F32)
    mid = r1.astype(BF16)
    lo = (r1 - mid.astype(F32)).astype(BF16)
    return hi, mid, lo


def _row_is_lat(i, bm, n_lat):
    row = i * bm + lax.broadcasted_iota(jnp.int32, (bm, 1), 0)
    return row < n_lat


def _ada_kernel(x_ref, w_ref, b_ref, o_ref):
    x = x_ref[...]
    xs = (x * _sigmoid(x)).astype(BF16)
    o_ref[0] = jnp.dot(xs, w_ref[0].astype(BF16), preferred_element_type=F32) + b_ref[0]


def _ada_mods(c, c_ctx, ada_w, ada_b):
    depth, d, n6 = ada_w.shape
    xin = jnp.zeros((SUBLANES, d), F32).at[0].set(c[0]).at[1].set(c_ctx)
    bn = _pick(n6, (512, 256, 128))
    return pl.pallas_call(
        _ada_kernel,
        grid=(depth, n6 // bn),
        in_specs=[pl.BlockSpec((SUBLANES, d), lambda l, j: (0, 0)),
                  pl.BlockSpec((1, d, bn), lambda l, j: (l, 0, j)),
                  pl.BlockSpec((1, 1, bn), lambda l, j: (l, 0, j))],
        out_specs=pl.BlockSpec((1, SUBLANES, bn), lambda l, j: (l, 0, j)),
        out_shape=jax.ShapeDtypeStruct((depth, SUBLANES, n6), F32),
        compiler_params=_cparams(("parallel", "parallel"), _vmem([((d, bn), F32), ((d, bn), BF16)])),
        name="ada_mods",
    )(xin, ada_w, ada_b.reshape(depth, 1, n6))


def _normmod_kernel(h_ref, nw_ref, sh_ref, sc_ref, o_ref, *, bm, n_lat):
    x = h_ref[...]
    y = x * lax.rsqrt(jnp.mean(x * x, axis=-1, keepdims=True) + RMS_EPS) * nw_ref[...]
    is_lat = _row_is_lat(pl.program_id(0), bm, n_lat)
    sh = jnp.where(is_lat, sh_ref[0:1, :], sh_ref[1:2, :])
    sc = jnp.where(is_lat, sc_ref[0:1, :], sc_ref[1:2, :])
    o_ref[...] = (y * (1.0 + sc) + sh).astype(BF16)


def _normmod(h, nw, mods, k_shift, k_scale, n_lat):
    t, d = h.shape
    bm = _pick(t, (256, 128))
    return pl.pallas_call(
        functools.partial(_normmod_kernel, bm=bm, n_lat=n_lat),
        grid=(t // bm,),
        in_specs=[pl.BlockSpec((bm, d), lambda i: (i, 0)),
                  pl.BlockSpec((1, d), lambda i: (0, 0)),
                  pl.BlockSpec((SUBLANES, d), lambda i: (0, k_shift)),
                  pl.BlockSpec((SUBLANES, d), lambda i: (0, k_scale))],
        out_specs=pl.BlockSpec((bm, d), lambda i: (i, 0)),
        out_shape=jax.ShapeDtypeStruct((t, d), BF16),
        compiler_params=_cparams(("parallel",), _vmem([((bm, d), F32), ((bm, d), BF16), ((bm, d), F32)])),
        name="norm_modulate",
    )(h, nw.reshape(1, d), mods, mods)


def _final_norm_kernel(h_ref, nw_ref, o_ref):
    x = h_ref[...]
    o_ref[...] = x * lax.rsqrt(jnp.mean(x * x, axis=-1, keepdims=True) + RMS_EPS) * nw_ref[...]


def _final_norm(h, nw, n_lat):
    t, d = h.shape
    bm = _pick(n_lat, (256, 128))
    return pl.pallas_call(
        _final_norm_kernel,
        grid=(n_lat // bm,),
        in_specs=[pl.BlockSpec((bm, d), lambda i: (i, 0)),
                  pl.BlockSpec((1, d), lambda i: (0, 0))],
        out_specs=pl.BlockSpec((bm, d), lambda i: (i, 0)),
        out_shape=jax.ShapeDtypeStruct((n_lat, d), F32),
        compiler_params=_cparams(("parallel",), _vmem([((bm, d), F32), ((bm, d), F32), ((bm, d), F32)])),
        name="final_norm",
    )(h, nw.reshape(1, d))


def _mm_plain_kernel(x_ref, w_ref, o_ref):
    o_ref[...] = jnp.dot(x_ref[...], w_ref[...], preferred_element_type=F32).astype(o_ref.dtype)


def _mm_plain(x, w, out_dtype, *, col_off=0, n_cols=None, bm=None, bn=None):
    t, k = x.shape
    n_cols = w.shape[1] - col_off if n_cols is None else n_cols
    bm = bm or _pick(t, (768, 512, 256, 128))
    bn = bn or _pick(math.gcd(n_cols, col_off) if col_off else n_cols, (512, 256, 128))
    ob = col_off // bn
    return pl.pallas_call(
        _mm_plain_kernel,
        grid=(t // bm, n_cols // bn),
        in_specs=[pl.BlockSpec((bm, k), lambda i, j: (i, 0)),
                  pl.BlockSpec((k, bn), lambda i, j: (0, j + ob))],
        out_specs=pl.BlockSpec((bm, bn), lambda i, j: (i, j)),
        out_shape=jax.ShapeDtypeStruct((t, n_cols), out_dtype),
        compiler_params=_cparams(("parallel", "arbitrary"),
                                 _vmem([((bm, k), BF16), ((k, bn), BF16), ((bm, bn), F32)])),
        name="matmul",
    )(x, w)


def _qkv_kernel(x_ref, w_ref, cos_ref, s1_ref, s2_ref, o_ref, *, nb_region, bn, q_scale):
    acc = jnp.dot(x_ref[...], w_ref[...], preferred_element_type=F32)
    region = pl.program_id(1) // nb_region

    @pl.when(region == 1)
    def _():
        o_ref[...] = acc.astype(BF16)

    @pl.when(region != 1)
    def _():
        reps = bn // LANES
        cos = jnp.tile(cos_ref[...], (1, reps))
        s1 = jnp.tile(s1_ref[...], (1, reps))
        s2 = jnp.tile(s2_ref[...], (1, reps))
        out = acc * cos + pltpu.roll(acc, 16, axis=1) * s1 + pltpu.roll(acc, bn - 16, axis=1) * s2
        scale = jnp.where(region == 2, q_scale, 1.0).astype(F32)
        o_ref[...] = (out * scale).astype(BF16)


def _qkv_proj(a, w_in, cos_t, s1_t, s2_t):
    t, k = a.shape
    bm = _pick(t, (768, 512, 256, 128))
    bn = _pick(A_WIDTH, (512, 256, 128))
    n_cols = 3 * A_WIDTH
    return pl.pallas_call(
        functools.partial(_qkv_kernel, nb_region=A_WIDTH // bn, bn=bn, q_scale=A_HEAD_DIM ** -0.5 * math.log2(math.e)),
        grid=(t // bm, n_cols // bn),
        in_specs=[pl.BlockSpec((bm, k), lambda i, j: (i, 0)),
                  pl.BlockSpec((k, bn), lambda i, j: (0, j)),
                  pl.BlockSpec((bm, LANES), lambda i, j: (i, 0)),
                  pl.BlockSpec((bm, LANES), lambda i, j: (i, 0)),
                  pl.BlockSpec((bm, LANES), lambda i, j: (i, 0))],
        out_specs=pl.BlockSpec((bm, bn), lambda i, j: (i, j)),
        out_shape=jax.ShapeDtypeStruct((t, n_cols), BF16),
        compiler_params=_cparams(("parallel", "arbitrary"),
                                 _vmem([((bm, k), BF16), ((k, bn), BF16), ((bm, bn), F32), ((bm, bn), F32)])),
        name="qkv_rope_proj",
    )(a, w_in, cos_t, s1_t, s2_t)


def _glu_kernel(x_ref, wa_ref, wg_ref, o_ref):
    x = x_ref[...]
    a = jnp.dot(x, wa_ref[...], preferred_element_type=F32)
    g = jnp.dot(x, wg_ref[...], preferred_element_type=F32)
    o_ref[...] = (a * _sigmoid(g)).astype(o_ref.dtype)


def _swiglu_up_kernel(x_ref, w1_ref, w3_ref, o_ref):
    x = x_ref[...]
    a = jnp.dot(x, w1_ref[...], preferred_element_type=F32)
    b = jnp.dot(x, w3_ref[...], preferred_element_type=F32)
    o_ref[...] = (a * _sigmoid(a) * b).astype(o_ref.dtype)


def _mm_pair(kern, x, wa, wg, out_dtype, *, off_a=0, off_g=0, n_cols=None, name="matmul_pair"):
    t, k = x.shape
    n_cols = wa.shape[1] if n_cols is None else n_cols
    bm = _pick(t, (768, 512, 256, 128))
    g = n_cols
    for o in (off_a, off_g):
        g = math.gcd(g, o) if o else g
    bn = _pick(g, (512, 256, 128))
    oa, og = off_a // bn, off_g // bn
    return pl.pallas_call(
        kern,
        grid=(t // bm, n_cols // bn),
        in_specs=[pl.BlockSpec((bm, k), lambda i, j: (i, 0)),
                  pl.BlockSpec((k, bn), lambda i, j: (0, j + oa)),
                  pl.BlockSpec((k, bn), lambda i, j: (0, j + og))],
        out_specs=pl.BlockSpec((bm, bn), lambda i, j: (i, j)),
        out_shape=jax.ShapeDtypeStruct((t, n_cols), out_dtype),
        compiler_params=_cparams(("parallel", "arbitrary"),
                                 _vmem([((bm, k), BF16), ((k, bn), BF16), ((k, bn), BF16), ((bm, bn), F32),
                                        ((bm, bn), F32)])),
        name=name,
    )(x, wa, wg)


def _resid1_kernel(x_ref, w_ref, h_ref, gate_ref, o_ref, *, bm, n_lat):
    acc = jnp.dot(x_ref[...], w_ref[...], preferred_element_type=F32)
    is_lat = _row_is_lat(pl.program_id(0), bm, n_lat)
    gate = jnp.where(is_lat, gate_ref[0:1, :], gate_ref[1:2, :])
    o_ref[...] = h_ref[...] + gate * acc


def _resid2_kernel(x1_ref, w1_ref, x2_ref, w2_ref, h_ref, gate_ref, o_ref, *, bm, n_lat):
    acc = jnp.dot(x1_ref[...], w1_ref[...], preferred_element_type=F32)
    acc = acc + jnp.dot(x2_ref[...], w2_ref[...], preferred_element_type=F32)
    is_lat = _row_is_lat(pl.program_id(0), bm, n_lat)
    gate = jnp.where(is_lat, gate_ref[0:1, :], gate_ref[1:2, :])
    o_ref[...] = h_ref[...] + gate * acc


def _mm_resid(xs, ws, h, mods, k_gate, n_lat, *, bn_cands=(512, 256, 128), name="matmul_resid"):
    t, d = h.shape
    bm = _pick(t, (768, 512, 256, 128))
    bn = _pick(d, bn_cands)
    nbd = d // bn
    in_specs, args, blocks = [], [], []
    for x, w in zip(xs, ws):
        k = x.shape[1]
        in_specs += [pl.BlockSpec((bm, k), lambda i, j: (i, 0)), pl.BlockSpec((k, bn), lambda i, j: (0, j))]
        args += [x, w]
        blocks += [((bm, k), BF16), ((k, bn), BF16)]
    in_specs += [pl.BlockSpec((bm, bn), lambda i, j: (i, j)),
                 pl.BlockSpec((SUBLANES, bn), lambda i, j: (0, k_gate * nbd + j))]
    args += [h, mods]
    blocks += [((bm, bn), F32), ((bm, bn), F32), ((bm, bn), F32)]
    kern = _resid1_kernel if len(xs) == 1 else _resid2_kernel
    return pl.pallas_call(
        functools.partial(kern, bm=bm, n_lat=n_lat),
        grid=(t // bm, d // bn),
        in_specs=in_specs,
        out_specs=pl.BlockSpec((bm, bn), lambda i, j: (i, j)),
        out_shape=jax.ShapeDtypeStruct((t, d), F32),
        compiler_params=_cparams(("parallel", "arbitrary"), _vmem(blocks)),
        name=name,
    )(*args)


def _attn_kernel(lam_ref, subw_ref, q_ref, k_ref, v_ref, o_ref, s_scr, p_scr, a_scr, acc_scr, m_scr,
                 *, bq, bk, n_lat, n_ctx, lam_init):
    qi = pl.program_id(1)
    q = q_ref[...]
    lane = lax.broadcasted_iota(jnp.int32, (bq, LANES), 1)
    zero = jnp.zeros_like(q)
    qm = jnp.concatenate([jnp.where(lane < A_HEAD_DIM, q, zero),
                          jnp.where(lane >= A_HEAD_DIM, q, zero)], axis=0)

    n_chunks = n_lat // bk
    rg = 64

    def scores(slot, kc):
        w = kc.shape[0]
        s_scr[slot, :, 0:w] = lax.dot_general(qm, kc, (((1,), (1,)), ((), ())), preferred_element_type=F32)

    def softmax(slot, w):
        for r0 in range(0, 2 * bq, rg):
            s = s_scr[slot, r0:r0 + rg, 0:w]
            m_old = m_scr[r0:r0 + rg, :]
            m_new = jnp.maximum(m_old, jnp.max(s, axis=1, keepdims=True))
            m_scr[r0:r0 + rg, :] = m_new
            a_scr[slot, r0:r0 + rg, :] = jnp.exp2(m_old - m_new)
            p_scr[slot, r0:r0 + rg, 0:w] = jnp.exp2(s - m_new).astype(BF16)

    def accumulate(slot, w, vc):
        vext = jnp.concatenate([vc, jnp.ones_like(vc)], axis=1)
        acc_scr[...] = (a_scr[slot] * acc_scr[...]
                        + jnp.dot(p_scr[slot, :, 0:w], vext, preferred_element_type=F32))

    def kchunk(c):
        return k_ref[pl.ds(pl.multiple_of(c * bk, bk), bk), :]

    def vchunk(c):
        return v_ref[pl.ds(pl.multiple_of(c * bk, bk), bk), :]

    m_scr[...] = jnp.full(m_scr.shape, NEG_BIG, F32)
    acc_scr[...] = jnp.zeros(acc_scr.shape, F32)
    scores(1, k_ref[n_lat:n_lat + n_ctx, :])
    scores(0, k_ref[0:bk, :])
    softmax(1, n_ctx)

    @pl.when(qi >= n_lat // bq)
    def _():
        accumulate(1, n_ctx, v_ref[n_lat:n_lat + n_ctx, :])

    @pl.when(qi < n_lat // bq)
    def _():
        accumulate(1, n_ctx, v_ref[n_lat:n_lat + n_ctx, :])
        softmax(0, bk)
        scores(1, k_ref[bk:2 * bk, :])

        def body(j, carry):
            accumulate(0, bk, vchunk(2 * j))
            softmax(1, bk)
            scores(0, kchunk(2 * j + 2))
            accumulate(1, bk, vchunk(2 * j + 1))
            softmax(0, bk)
            scores(1, kchunk(2 * j + 3))
            return carry

        lax.fori_loop(0, n_chunks // 2 - 1, body, 0)
        accumulate(0, bk, v_ref[(n_chunks - 2) * bk:(n_chunks - 1) * bk, :])
        softmax(1, bk)
        accumulate(1, bk, v_ref[(n_chunks - 1) * bk:n_chunks * bk, :])

    acc = acc_scr[...]
    lam = lam_ref[...]
    lam_full = (jnp.exp(jnp.sum(lam[0:1] * lam[1:2], axis=1, keepdims=True))
                - jnp.exp(jnp.sum(lam[2:3] * lam[3:4], axis=1, keepdims=True)) + lam_init)
    o = acc[:bq, :LANES] / acc[:bq, LANES:] - lam_full * (acc[bq:, :LANES] / acc[bq:, LANES:])
    y = o * lax.rsqrt(jnp.mean(o * o, axis=-1, keepdims=True) + 1e-5) * subw_ref[...]
    o_ref[...] = (y * (1.0 - lam_init)).astype(BF16)


def _attention(kvq, lam, subw, n_lat, n_ctx, lam_init):
    t = kvq.shape[0]
    bq = _pick(math.gcd(n_lat, n_ctx), (256, 128))
    bk = _pick(n_lat // 2, (512, 256, 128))
    nh = A_HEADS
    return pl.pallas_call(
        functools.partial(_attn_kernel, bq=bq, bk=bk, n_lat=n_lat, n_ctx=n_ctx, lam_init=lam_init),
        grid=(nh, t // bq),
        in_specs=[pl.BlockSpec((4, A_HEAD_DIM), lambda h, i: (0, 0)),
                  pl.BlockSpec((1, LANES), lambda h, i: (0, 0)),
                  pl.BlockSpec((bq, LANES), lambda h, i: (i, 2 * nh + h)),
                  pl.BlockSpec((t, LANES), lambda h, i: (0, h)),
                  pl.BlockSpec((t, LANES), lambda h, i: (0, nh + h))],
        out_specs=pl.BlockSpec((bq, LANES), lambda h, i: (i, h)),
        out_shape=jax.ShapeDtypeStruct((t, A_WIDTH), BF16),
        scratch_shapes=[pltpu.VMEM((2, 2 * bq, bk), F32), pltpu.VMEM((2, 2 * bq, bk), BF16),
                        pltpu.VMEM((2, 2 * bq, 1), F32), pltpu.VMEM((2 * bq, 2 * LANES), F32),
                        pltpu.VMEM((2 * bq, 1), F32)],
        compiler_params=_cparams(("parallel", "arbitrary"),
                                 _vmem([((t, LANES), BF16), ((t, LANES), BF16)], scratch=16 << 20)),
        name="diff_attention",
    )(lam, subw.reshape(1, LANES), kvq, kvq, kvq)


def _dwconv_kernel(prev_ref, cur_ref, next_ref, w_ref, b_ref, lnw_ref, lnb_ref, o_ref, xx_ref, y_ref,
                   *, bm, cb, taps, nb_lat, nb_all, layernorm):
    i = pl.program_id(0)
    has_prev = jnp.logical_and(i != 0, i != nb_lat)
    has_next = jnp.logical_and(i != nb_lat - 1, i != nb_all - 1)
    xx_ref[0:HALO, :] = jnp.where(has_prev, prev_ref[...], 0.0)
    xx_ref[HALO:HALO + bm, :] = cur_ref[...]
    xx_ref[HALO + bm:2 * HALO + bm, :] = jnp.where(has_next, next_ref[...], 0.0)
    pad = (taps - 1) // 2
    rs = min(bm, 64)
    cs = min(cb, 512)
    for c0 in range(0, cb, cs):
        wts = [w_ref[k:k + 1, c0:c0 + cs] for k in range(taps)]
        bias = b_ref[:, c0:c0 + cs]
        for r0 in range(0, bm, rs):
            acc = jnp.broadcast_to(bias, (rs, cs))
            for k in range(taps):
                lo = r0 + HALO + k - pad
                acc = acc + xx_ref[lo:lo + rs, c0:c0 + cs] * wts[k]
            y_ref[r0:r0 + rs, c0:c0 + cs] = acc
    y = y_ref[...]
    if layernorm:
        mu = jnp.mean(y, axis=-1, keepdims=True)
        var = jnp.mean(jnp.square(y - mu), axis=-1, keepdims=True)
        y = (y - mu) * lax.rsqrt(var + LN_EPS) * lnw_ref[...] + lnb_ref[...]
    o_ref[...] = (y * _sigmoid(y)).astype(o_ref.dtype)


def _dwconv_silu(x, w, b, n_lat, out_dtype, *, col_off=0, n_cols=None, ln=None):
    t = x.shape[0]
    taps = w.shape[0]
    n_cols = x.shape[1] - col_off if n_cols is None else n_cols
    bm = _pick(math.gcd(n_lat, t - n_lat), (256, 128))
    cb = n_cols if ln is not None else _pick(math.gcd(n_cols, col_off) if col_off else n_cols, (1024, 512, 256, 128))
    ob = col_off // cb
    hb = bm // HALO
    n_halo_blocks = t // HALO
    lnw, lnb = ln if ln is not None else (jnp.ones((n_cols,), F32), jnp.zeros((n_cols,), F32))
    return pl.pallas_call(
        functools.partial(_dwconv_kernel, bm=bm, cb=cb, taps=taps, nb_lat=n_lat // bm, nb_all=t // bm,
                          layernorm=ln is not None),
        grid=(t // bm, n_cols // cb),
        in_specs=[pl.BlockSpec((HALO, cb), lambda i, j: (jnp.maximum(i * hb - 1, 0), j + ob)),
                  pl.BlockSpec((bm, cb), lambda i, j: (i, j + ob)),
                  pl.BlockSpec((HALO, cb), lambda i, j: (jnp.minimum((i + 1) * hb, n_halo_blocks - 1), j + ob)),
                  pl.BlockSpec((taps, cb), lambda i, j: (0, j)),
                  pl.BlockSpec((1, cb), lambda i, j: (0, j)),
                  pl.BlockSpec((1, cb), lambda i, j: (0, j)),
                  pl.BlockSpec((1, cb), lambda i, j: (0, j))],
        out_specs=pl.BlockSpec((bm, cb), lambda i, j: (i, j)),
        out_shape=jax.ShapeDtypeStruct((t, n_cols), out_dtype),
        scratch_shapes=[pltpu.VMEM((bm + 2 * HALO, cb), F32), pltpu.VMEM((bm, cb), F32)],
        compiler_params=_cparams(("parallel", "parallel"),
                                 _vmem([((bm, cb), F32), ((bm, cb), F32)], scratch=3 * bm * cb * 4)),
        name="dwconv_silu",
    )(x, x, x, w, b.reshape(1, n_cols), lnw.reshape(1, n_cols), lnb.reshape(1, n_cols))


def _s5_build_kernel(u_ref, lam_ref, step_ref, bt_ref, c_ref, yz_ref, et_ref, adv_ref):
    L, H, P = S5_CHUNK, S5_GROUP, S5_STATE
    lane = lax.broadcasted_iota(jnp.int32, (1, 2 * P), 1)
    first = lane < P
    sgn = jnp.where(first, -1.0, 1.0).astype(F32)
    conj_sgn = -sgn

    def swap(x):
        return pltpu.roll(x, P, axis=1)

    def dupr(x):
        return jnp.where(first, x, swap(x))

    def dupi(x):
        return jnp.where(first, swap(x), x)

    def cmul(a, b):
        return dupr(a) * b + (sgn * dupi(a)) * swap(b)

    one = jnp.where(first, 1.0, 0.0).astype(F32)
    mats, ets, advs = [], [], []
    for d in range(2):
        lam = lam_ref[0, d:d + 1, :]
        z = lam * jnp.exp(step_ref[0, d:d + 1, :])
        ang = dupi(z)
        lbar = jnp.exp(dupr(z)) * jnp.where(first, jnp.cos(ang), jnp.sin(ang))
        den = dupr(lam) * dupr(lam) + dupi(lam) * dupi(lam)
        qcoef = cmul(lbar - one, lam * conj_sgn) / den
        bbar = cmul(qcoef, bt_ref[0])
        cmat = c_ref[0, d * H:(d + 1) * H, :]
        pw = [one]
        for _ in range(L):
            pw.append(cmul(pw[-1], lbar))
        cpow = [cmul(pw[t], cmat) * conj_sgn for t in range(L + 1)]
        if d == 0:
            kern_rows = jnp.concatenate([cpow[t] for t in range(L)], axis=0)
            inj = jnp.concatenate([cmul(pw[L - 1 - j], bbar) for j in range(L)], axis=0)
            read = jnp.concatenate([cpow[t + 1] for t in range(L)], axis=0)
        else:
            kern_rows = jnp.concatenate([cpow[L - 1 - t] for t in range(L)], axis=0)
            inj = jnp.concatenate([cmul(pw[j], bbar) for j in range(L)], axis=0)
            read = jnp.concatenate([cpow[L - t] for t in range(L)], axis=0)
        base = lax.dot_general(bbar.astype(BF16), kern_rows.astype(BF16), (((1,), (1,)), ((), ())),
                               preferred_element_type=F32)
        lane_o = lax.broadcasted_iota(jnp.int32, (H, L * H), 1)
        blocks = []
        for j in range(L):
            if d == 0:
                sh = (j * H) % (L * H)
                blk = jnp.where(lane_o >= j * H, pltpu.roll(base, sh, axis=1) if sh else base, 0.0)
            else:
                sh = (L * H - (L - 1 - j) * H) % (L * H)
                blk = jnp.where(lane_o < (j + 1) * H, pltpu.roll(base, sh, axis=1) if sh else base, 0.0)
            blocks.append(blk)
        mats.append((jnp.concatenate(blocks, axis=0), inj))
        ets.append(read)
        pl_ = pw[L]
        advs.append((dupr(pl_), sgn * dupi(pl_)))
    mix = (mats[0][0] + mats[1][0]).astype(BF16)
    rhs = jnp.concatenate([mix, mats[0][1].astype(BF16), mats[1][1].astype(BF16)], axis=1)
    yz_ref[0] = jnp.dot(u_ref[0], rhs, preferred_element_type=F32)
    et_ref[0] = jnp.concatenate([ets[0], ets[1]], axis=1).astype(BF16)
    adv_ref[0] = jnp.concatenate([jnp.concatenate([advs[0][0], advs[1][0]], axis=1),
                                  jnp.concatenate([advs[0][1], advs[1][1]], axis=1)], axis=0)


def _s5_scan_kernel(z_ref, adv_ref, s_ref, *, n_chunks, nc_ctx):
    P = S5_STATE
    a1 = adv_ref[:, 0, :]
    a2 = adv_ref[:, 1, :]
    gb = a1.shape[0]
    nl = n_chunks - nc_ctx

    def swap_halves(x):
        return jnp.concatenate([pltpu.roll(x[:, :2 * P], P, axis=1), pltpu.roll(x[:, 2 * P:], P, axis=1)], axis=1)

    def body(i, s):
        cf = jnp.where(i < nc_ctx, nl + i, i - nc_ctx)
        cb = n_chunks - 1 - i
        zf = z_ref[cf][:, :2 * P]
        zb = z_ref[cb][:, 2 * P:]
        s_ref[cf, :, 0:2 * P] = s[:, :2 * P].astype(s_ref.dtype)
        s_ref[cb, :, 2 * P:4 * P] = s[:, 2 * P:].astype(s_ref.dtype)
        return a1 * s + a2 * swap_halves(s) + jnp.concatenate([zf, zb], axis=1)

    lax.fori_loop(0, n_chunks, body, jnp.zeros((gb, 4 * P), F32))


def _s5_out_kernel(yz_ref, s_ref, et_ref, o_ref):
    n = o_ref.shape[2]
    o_ref[0] = yz_ref[0, :, 0:n] + lax.dot_general(s_ref[0], et_ref[0], (((1,), (1,)), ((), ())),
                                                  preferred_element_type=F32)


def _s5_finish_kernel(y_ref, u_ref, d_ref, w_ref, b_ref, o_ref):
    y = y_ref[...] + d_ref[...] * u_ref[...]
    g = 0.5 * y * (1.0 + jnp.tanh(math.sqrt(2.0 / math.pi) * (y + 0.044715 * (y * y * y))))
    gate = jnp.dot(g.astype(BF16), w_ref[...], preferred_element_type=F32) + b_ref[...]
    o_ref[...] = (g * _sigmoid(gate)).astype(BF16)


def _s5_branch(u, lam_re, lam_im, log_step, b_re, b_im, c_re, c_im, d_skip, glu_w, glu_b, n_lat):
    t = u.shape[0]
    G, H, P, L = S5_GROUPS, S5_GROUP, S5_STATE, S5_CHUNK
    nch = t // L
    nc_ctx = (t - n_lat) // L
    lam_p = jnp.concatenate([lam_re, lam_im], axis=-1).transpose(1, 0, 2)
    step_p = jnp.broadcast_to(log_step.T[:, :, None], (G, 2, 2 * P))
    bt_p = jnp.concatenate([b_re, b_im], axis=1).transpose(0, 2, 1)
    c_p = jnp.concatenate([c_re, c_im], axis=-1).transpose(1, 0, 2, 3).reshape(G, 2 * H, 2 * P)
    ug = u.astype(BF16).reshape(nch, L, G, H).transpose(2, 0, 1, 3).reshape(G, nch, L * H)
    LH = L * H
    yz, et, adv = pl.pallas_call(
        _s5_build_kernel,
        grid=(G,),
        in_specs=[pl.BlockSpec((1, nch, LH), lambda g: (g, 0, 0)),
                  pl.BlockSpec((1, 2, 2 * P), lambda g: (g, 0, 0)),
                  pl.BlockSpec((1, 2, 2 * P), lambda g: (g, 0, 0)),
                  pl.BlockSpec((1, H, 2 * P), lambda g: (g, 0, 0)),
                  pl.BlockSpec((1, 2 * H, 2 * P), lambda g: (g, 0, 0))],
        out_specs=[pl.BlockSpec((1, nch, LH + 4 * P), lambda g: (g, 0, 0)),
                   pl.BlockSpec((1, LH, 4 * P), lambda g: (g, 0, 0)),
                   pl.BlockSpec((1, 2, 4 * P), lambda g: (g, 0, 0))],
        out_shape=[jax.ShapeDtypeStruct((G, nch, LH + 4 * P), F32),
                   jax.ShapeDtypeStruct((G, LH, 4 * P), BF16),
                   jax.ShapeDtypeStruct((G, 2, 4 * P), F32)],
        compiler_params=_cparams(("parallel",), _vmem([((nch, LH), BF16), ((nch, LH + 4 * P), F32)])),
        name="s5_build_apply",
    )(ug, lam_p, step_p, bt_p, c_p)
    z_t = yz[:, :, LH:].transpose(1, 0, 2)
    gb = _pick(G, (16, 8))
    s_in = pl.pallas_call(
        functools.partial(_s5_scan_kernel, n_chunks=nch, nc_ctx=nc_ctx),
        grid=(G // gb,),
        in_specs=[pl.BlockSpec((nch, gb, 4 * P), lambda g: (0, g, 0)),
                  pl.BlockSpec((gb, 2, 4 * P), lambda g: (g, 0, 0))],
        out_specs=pl.BlockSpec((nch, gb, 4 * P), lambda g: (0, g, 0)),
        out_shape=jax.ShapeDtypeStruct((nch, G, 4 * P), BF16),
        compiler_params=_cparams(("parallel",), _vmem([((nch, gb, 4 * P), F32), ((nch, gb, 4 * P), BF16)])),
        name="s5_chunk_scan",
    )(z_t, adv)
    s_g = s_in.transpose(1, 0, 2)
    y = pl.pallas_call(
        _s5_out_kernel,
        grid=(G,),
        in_specs=[pl.BlockSpec((1, nch, LH + 4 * P), lambda g: (g, 0, 0)),
                  pl.BlockSpec((1, nch, 4 * P), lambda g: (g, 0, 0)),
                  pl.BlockSpec((1, LH, 4 * P), lambda g: (g, 0, 0))],
        out_specs=pl.BlockSpec((1, nch, LH), lambda g: (g, 0, 0)),
        out_shape=jax.ShapeDtypeStruct((G, nch, LH), F32),
        compiler_params=_cparams(("parallel",), _vmem([((nch, LH + 4 * P), F32), ((nch, 4 * P), BF16),
                                                       ((nch, LH), F32)])),
        name="s5_readout",
    )(yz, s_g, et)
    y_tok = y.reshape(G, nch, L, H).transpose(1, 2, 0, 3).reshape(t, G * H)
    w = G * H
    bm = _pick(t, (256, 128))
    return pl.pallas_call(
        _s5_finish_kernel,
        grid=(t // bm,),
        in_specs=[pl.BlockSpec((bm, w), lambda i: (i, 0)),
                  pl.BlockSpec((bm, w), lambda i: (i, 0)),
                  pl.BlockSpec((1, w), lambda i: (0, 0)),
                  pl.BlockSpec((w, w), lambda i: (0, 0)),
                  pl.BlockSpec((1, w), lambda i: (0, 0))],
        out_specs=pl.BlockSpec((bm, w), lambda i: (i, 0)),
        out_shape=jax.ShapeDtypeStruct((t, w), BF16),
        compiler_params=_cparams(("parallel",), _vmem([((bm, w), F32), ((bm, w), F32), ((w, w), BF16),
                                                       ((bm, w), F32)])),
        name="s5_gelu_glu",
    )(y_tok, u, d_skip.reshape(1, w), glu_w.astype(BF16), glu_b.reshape(1, w))


def _ssd_chunk(forward, xs_ref, b_ref, c_ref, dt_ref, sel_ref, bias_ref, alog_ref, y_ref, state_ref, hpg):
    L = SSD_CHUNK
    hd = M2_HEAD_DIM
    sel = sel_ref[0]
    hi, mid, lo = _split3(dt_ref[...])
    dt_raw = (jnp.dot(hi, sel, preferred_element_type=F32) + jnp.dot(mid, sel, preferred_element_type=F32)
              + jnp.dot(lo, sel, preferred_element_type=F32))
    v = dt_raw + bias_ref[0]
    dt = jnp.maximum(v, 0.0) + jnp.log1p(jnp.exp(-jnp.abs(v)))
    da = dt * (-jnp.exp(alog_ref[0]))
    ti = lax.broadcasted_iota(jnp.int32, (L, L), 0)
    si = lax.broadcasted_iota(jnp.int32, (L, L), 1)
    tri = jnp.where(si <= ti, 1.0, 0.0).astype(BF16)
    hi, mid, lo = _split3(da)
    cum = (jnp.dot(tri, hi, preferred_element_type=F32) + jnp.dot(tri, mid, preferred_element_type=F32)
           + jnp.dot(tri, lo, preferred_element_type=F32))
    total = cum[L - 1:L, :]
    if forward:
        cumq = cum
        w_out = jnp.exp(cumq)
        w_state = jnp.exp(total - cumq)
        mask = si <= ti
    else:
        cumq = cum - da
        w_out = jnp.exp(total - cumq)
        w_state = jnp.exp(cumq)
        mask = si >= ti
    xsdt = xs_ref[...] * dt
    cb_ = c_ref[...]
    bb_ = b_ref[...]
    s_old = state_ref[...]
    y_off = w_out * jnp.dot(cb_, s_old.astype(BF16), preferred_element_type=F32)
    state_ref[...] = jnp.exp(total) * s_old + lax.dot_general(
        bb_, (w_state * xsdt).astype(BF16), (((0,), (0,)), ((), ())), preferred_element_type=F32)
    cbm = lax.dot_general(cb_, bb_, (((1,), (1,)), ((), ())), preferred_element_type=F32)
    cum_t = cumq.T
    lane = lax.broadcasted_iota(jnp.int32, (L, LANES), 1)
    xb = xsdt.astype(BF16)
    zero = jnp.zeros((L, LANES), BF16)
    outs = []
    for pr in range(hpg // 2):
        gs = []
        for r in (2 * pr, 2 * pr + 1):
            col = cumq[:, hd * r:hd * r + 1]
            row = cum_t[hd * r:hd * r + 1, :]
            diff = (col - row) if forward else (row - col)
            decay = jnp.exp(jnp.where(mask, diff, NEG_BIG))
            gs.append((cbm * decay).astype(BF16))
        xp = xb[:, LANES * pr:LANES * (pr + 1)]
        rhs = jnp.concatenate([jnp.where(lane < hd, xp, zero), jnp.where(lane >= hd, xp, zero)], axis=0)
        outs.append(jnp.dot(jnp.concatenate(gs, axis=1), rhs, preferred_element_type=F32))
    y_ref[...] = jnp.concatenate(outs, axis=1) + y_off


def _ssd_kernel(*refs, hpg):
    fwd, bwd, (yf_ref, yb_ref, sf_ref, sb_ref) = refs[0:7], refs[7:14], refs[14:18]

    @pl.when(pl.program_id(1) == 0)
    def _():
        sf_ref[...] = jnp.zeros_like(sf_ref)
        sb_ref[...] = jnp.zeros_like(sb_ref)

    _ssd_chunk(True, *fwd, yf_ref, sf_ref, hpg)
    _ssd_chunk(False, *bwd, yb_ref, sb_ref, hpg)


def _ssd_finish_kernel(yf_ref, yb_ref, xs_ref, z_ref, d_ref, nw_ref, o_ref, *, n_groups):
    y = yf_ref[...] + yb_ref[...] + d_ref[...] * xs_ref[...]
    z = z_ref[...]
    y = y * (z * _sigmoid(z))
    gw = y.shape[1] // n_groups
    parts = []
    for g in range(n_groups):
        seg = y[:, g * gw:(g + 1) * gw]
        parts.append(seg * lax.rsqrt(jnp.mean(seg * seg, axis=-1, keepdims=True) + RMS_EPS))
    o_ref[...] = (jnp.concatenate(parts, axis=1) * nw_ref[...]).astype(BF16)


def _ssd_branch(xs, bm_, cm_, dt_raw, z, a_log, dt_bias, d_skip, norm_w, n_lat):
    t, inner = xs.shape
    L = SSD_CHUNK
    ng = M2_GROUPS
    hpg = M2_HEADS // ng
    w = hpg * M2_HEAD_DIM
    nch = t // L
    nc_ctx = (t - n_lat) // L
    nl = nch - nc_ctx
    src = (jnp.arange(2)[:, None, None] * M2_HEADS + jnp.arange(ng)[None, :, None] * hpg
           + jnp.arange(w)[None, None, :] // M2_HEAD_DIM)
    sel = (jnp.arange(LANES)[None, None, :, None] == src[:, :, None, :]).astype(BF16)
    expand = lambda p: jnp.repeat(p, M2_HEAD_DIM, axis=-1).reshape(2, ng, 1, w)
    bias_e = expand(dt_bias)
    alog_e = expand(a_log)

    order = (lambda c: jnp.where(c < nc_ctx, nl + c, c - nc_ctx), lambda c: nch - 1 - c)
    in_specs, args = [], []
    for d in range(2):
        ch = order[d]
        in_specs += [pl.BlockSpec((L, w), lambda g, c, ch=ch: (ch(c), g)),
                     pl.BlockSpec((L, M2_STATE), lambda g, c, ch=ch: (ch(c), g)),
                     pl.BlockSpec((L, M2_STATE), lambda g, c, ch=ch: (ch(c), g)),
                     pl.BlockSpec((L, LANES), lambda g, c, ch=ch: (ch(c), 0)),
                     pl.BlockSpec((1, LANES, w), lambda g, c: (g, 0, 0)),
                     pl.BlockSpec((1, 1, w), lambda g, c: (g, 0, 0)),
                     pl.BlockSpec((1, 1, w), lambda g, c: (g, 0, 0))]
        args += [xs, bm_, cm_, dt_raw, sel[d], bias_e[d], alog_e[d]]
    yf, yb = pl.pallas_call(
        functools.partial(_ssd_kernel, hpg=hpg),
        grid=(ng, nch),
        in_specs=in_specs,
        out_specs=[pl.BlockSpec((L, w), lambda g, c: (order[0](c), g)),
                   pl.BlockSpec((L, w), lambda g, c: (order[1](c), g))],
        out_shape=[jax.ShapeDtypeStruct((t, inner), F32), jax.ShapeDtypeStruct((t, inner), F32)],
        scratch_shapes=[pltpu.VMEM((M2_STATE, w), F32), pltpu.VMEM((M2_STATE, w), F32)],
        compiler_params=_cparams(("parallel", "arbitrary"), 32 << 20),
        name="ssd_chunk_scan",
    )(*args)
    bm = _pick(t, (256, 128))
    d_e = jnp.repeat(d_skip, M2_HEAD_DIM).reshape(1, inner)
    return pl.pallas_call(
        functools.partial(_ssd_finish_kernel, n_groups=ng),
        grid=(t // bm,),
        in_specs=[pl.BlockSpec((bm, inner), lambda i: (i, 0)),
                  pl.BlockSpec((bm, inner), lambda i: (i, 0)),
                  pl.BlockSpec((bm, inner), lambda i: (i, 0)),
                  pl.BlockSpec((bm, inner), lambda i: (i, 0)),
                  pl.BlockSpec((1, inner), lambda i: (0, 0)),
                  pl.BlockSpec((1, inner), lambda i: (0, 0))],
        out_specs=pl.BlockSpec((bm, inner), lambda i: (i, 0)),
        out_shape=jax.ShapeDtypeStruct((t, inner), BF16),
        compiler_params=_cparams(("parallel",), _vmem([((bm, inner), F32)] * 5)),
        name="ssd_gate_norm",
    )(yf, yb, xs, z, d_e, norm_w.reshape(1, inner))


def _rope_tables(n_lat, n_ctx):
    n_rows = n_lat // GRID_W
    rows = jnp.repeat(jnp.arange(n_rows), GRID_W).astype(F32)
    cols = jnp.tile(jnp.arange(GRID_W), n_rows).astype(F32)
    inv = jnp.power(ROPE_BASE, -jnp.arange(0, ROPE_AXIS_DIM, 2, dtype=F32) / ROPE_AXIS_DIM)
    ang_r = rows[:, None] * inv
    ang_c = cols[:, None] * inv
    ang = jnp.concatenate([ang_r, ang_r, ang_c, ang_c], axis=-1)
    cos, sin = jnp.cos(ang), jnp.sin(ang)
    half = ROPE_AXIS_DIM // 2
    first = (jnp.arange(A_HEAD_DIM) % ROPE_AXIS_DIM) < half
    s1 = jnp.where(first, 0.0, sin)
    s2 = jnp.where(first, -sin, 0.0)
    pad = lambda tbl, fill: jnp.concatenate([tbl, jnp.full((n_ctx, A_HEAD_DIM), fill, F32)], axis=0)
    two = lambda tbl: jnp.concatenate([tbl, tbl], axis=-1)
    return two(pad(cos, 1.0)), two(pad(s1, 0.0)), two(pad(s2, 0.0))


def _swiglu_ffn(h, mods, nw, w1, w3, w2, n_lat):
    a = _normmod(h, nw, mods, 3, 4, n_lat)
    hid = _mm_pair(_swiglu_up_kernel, a, w1, w3, BF16, name="ffn_gate_up")
    return _mm_resid([hid], [w2], h, mods, 5, n_lat, bn_cands=(256, 128), name="ffn_down_resid")


def _even_layer(h, mods, nw, w_in, w_out, lam, subw, conv_w, conv_b, ln_w, ln_b, rope, layer_idx, n_lat):
    n_ctx = h.shape[0] - n_lat
    a = _normmod(h, nw, mods, 0, 1, n_lat)
    w_in = w_in.astype(BF16)
    kvq = _qkv_proj(a, w_in, *rope)
    glu = _mm_pair(_glu_kernel, a, w_in, w_in, F32, off_a=EV_B0, off_g=EV_B0 + B_WIDTH, n_cols=B_WIDTH,
                   name="conv_glu_proj")
    lam_init = 0.8 - 0.6 * math.exp(-0.3 * layer_idx)
    att = _attention(kvq, lam, subw, n_lat, n_ctx, lam_init)
    cv = _dwconv_silu(glu, conv_w, conv_b, n_lat, BF16, ln=(ln_w, ln_b))
    w_out = w_out.astype(BF16)
    return _mm_resid([att, cv], [w_out[:A_WIDTH], w_out[A_WIDTH:]], h, mods, 2, n_lat, name="even_out_resid")


def _odd_layer(h, mods, nw, w_in, w_out, s5p, m2p, n_lat):
    a = _normmod(h, nw, mods, 0, 1, n_lat)
    w_main = w_in[:, :OD_DT0].astype(BF16)
    u = _mm_plain(a, w_main, F32, col_off=0, n_cols=S5_WIDTH)
    xbc = _mm_plain(a, w_main, F32, col_off=S5_WIDTH, n_cols=M2_CONV_DIM)
    n_dt = 2 * M2_HEADS
    w_dt = jnp.pad(w_in[:, OD_DT0:OD_Z0], ((0, 0), (0, LANES - n_dt))).astype(BF16)
    dt_raw = _mm_plain(a, w_dt, F32)
    z = _mm_plain(a, w_in[:, OD_Z0:].astype(BF16), F32)
    s5_out = _s5_branch(u, *s5p, n_lat)
    conv_w, conv_b, a_log, dt_bias, m2_d, m2_norm_w = m2p
    gn = M2_GROUPS * M2_STATE
    xs = _dwconv_silu(xbc, conv_w[:, :M2_INNER], conv_b[:M2_INNER], n_lat, F32, col_off=0, n_cols=M2_INNER)
    bmat = _dwconv_silu(xbc, conv_w[:, M2_INNER:M2_INNER + gn], conv_b[M2_INNER:M2_INNER + gn], n_lat, BF16,
                        col_off=M2_INNER, n_cols=gn)
    cmat = _dwconv_silu(xbc, conv_w[:, M2_INNER + gn:], conv_b[M2_INNER + gn:], n_lat, BF16,
                        col_off=M2_INNER + gn, n_cols=gn)
    ssd_out = _ssd_branch(xs, bmat, cmat, dt_raw, z, a_log, dt_bias, m2_d, m2_norm_w, n_lat)
    w_out = w_out.astype(BF16)
    return _mm_resid([s5_out, ssd_out], [w_out[:S5_WIDTH], w_out[S5_WIDTH:]], h, mods, 2, n_lat,
                     name="odd_out_resid")


def kernel(x, c, ctx, c_ctx, ada_w, ada_b, norm_w, ffn_w1, ffn_w3, ffn_w2, ev_w_in, ev_w_out, ev_lambda, ev_subln_w, ev_conv_w, ev_conv_b, ev_ln_w, ev_ln_b, od_w_in, od_w_out, s5_lam_re, s5_lam_im, s5_log_step, s5_b_re, s5_b_im, s5_c_re, s5_c_im, s5_d, s5_glu_w, s5_glu_b, m2_conv_w, m2_conv_b, m2_a_log, m2_dt_bias, m2_d, m2_norm_w, final_norm_w):
    assert x.shape[0] == 1 and c.shape[0] == 1 and ctx.shape[0] == 1
    n_lat, n_ctx = x.shape[1], ctx.shape[1]
    depth = ada_w.shape[0]
    h = jnp.concatenate([x[0], ctx[0]], axis=0)
    mods_all = _ada_mods(c, c_ctx, ada_w, ada_b)
    rope = _rope_tables(n_lat, n_ctx)
    for i in range(depth):
        mods = mods_all[i]
        j = i // 2
        if i % 2 == 0:
            h = _even_layer(h, mods, norm_w[i, 0], ev_w_in[j], ev_w_out[j], ev_lambda[j], ev_subln_w[j],
                            ev_conv_w[j], ev_conv_b[j], ev_ln_w[j], ev_ln_b[j], rope, i, n_lat)
        else:
            s5p = (s5_lam_re[j], s5_lam_im[j], s5_log_step[j], s5_b_re[j], s5_b_im[j], s5_c_re[j], s5_c_im[j],
                   s5_d[j], s5_glu_w[j], s5_glu_b[j])
            m2p = (m2_conv_w[j], m2_conv_b[j], m2_a_log[j], m2_dt_bias[j], m2_d[j], m2_norm_w[j])
            h = _odd_layer(h, mods, norm_w[i, 0], od_w_in[j], od_w_out[j], s5p, m2p, n_lat)
        h = _swiglu_ffn(h, mods, norm_w[i, 1], ffn_w1[i].astype(BF16), ffn_w3[i].astype(BF16),
                        ffn_w2[i].astype(BF16), n_lat)
    return _final_norm(h, final_norm_w, n_lat)[None]
```

```python
import functools
import math

import jax
import jax.numpy as jnp
from jax import lax
from jax.experimental import pallas as pl
from jax.experimental.pallas import tpu as pltpu

F32 = jnp.float32
BF16 = jnp.bfloat16

D_MODEL = 4096
SEQ = 8192
DEPTH = 2
CTX_LEN = 256
GRID_W = 64
RMS_EPS = 1e-6
LN_EPS = 1e-5
A_WIDTH = D_MODEL // 2
A_HEAD_DIM = 64
A_HEADS = A_WIDTH // (2 * A_HEAD_DIM)
B_WIDTH = D_MODEL - A_WIDTH
CONV_WIDTH = 31
ROPE_BASE = 10000.0
ROPE_AXIS_DIM = A_HEAD_DIM // 2
EV_V0 = A_WIDTH
EV_Q0 = 2 * A_WIDTH
EV_B0 = 3 * A_WIDTH
S5_WIDTH = D_MODEL // 4
S5_GROUP = 16
S5_GROUPS = S5_WIDTH // S5_GROUP
S5_STATE = 64
M2_INNER = D_MODEL - S5_WIDTH
M2_HEAD_DIM = 64
M2_HEADS = M2_INNER // M2_HEAD_DIM
M2_GROUPS = 8
M2_STATE = 128
M2_CONV = 5
M2_CONV_DIM = M2_INNER + 2 * M2_GROUPS * M2_STATE
OD_DT0 = S5_WIDTH + M2_CONV_DIM
OD_Z0 = OD_DT0 + 2 * M2_HEADS

V7X_VMEM_BYTES = 64 * 1024 * 1024
LANES = 128
SUBLANES = 8
HALO = 16
S5_CHUNK = 16
SSD_CHUNK = 128
NEG_BIG = -1e30


def _cparams(sem, vmem_bytes):
    return pltpu.CompilerParams(dimension_semantics=sem,
                                vmem_limit_bytes=int(min(vmem_bytes, V7X_VMEM_BYTES - (4 << 20))))


def _pick(n, cands):
    for c in cands:
        if n % c == 0:
            return c
    raise ValueError(f"no block size in {cands} divides {n}")


def _nbytes(shape, dtype):
    return math.prod(shape) * jnp.dtype(dtype).itemsize


def _vmem(blocks, scratch=0):
    return 2 * sum(_nbytes(s, d) for s, d in blocks) + scratch + (12 << 20)


def _sigmoid(x):
    return jax.nn.sigmoid(x)


def _split3(x):
    hi = x.astype(BF16)
    r1 = x - hi.astype(F32)
    mid = r1.astype(BF16)
    lo = (r1 - mid.astype(F32)).astype(BF16)
    return hi, mid, lo


def _row_is_lat(i, bm, n_lat):
    row = i * bm + lax.broadcasted_iota(jnp.int32, (bm, 1), 0)
    return row < n_lat


def _ada_kernel(x_ref, w_ref, b_ref, o_ref):
    x = x_ref[...]
    xs = (x * _sigmoid(x)).astype(BF16)
    o_ref[0] = jnp.dot(xs, w_ref[0].astype(BF16), preferred_element_type=F32) + b_ref[0]


def _ada_mods(c, c_ctx, ada_w, ada_b):
    depth, d, n6 = ada_w.shape
    xin = jnp.zeros((SUBLANES, d), F32).at[0].set(c[0]).at[1].set(c_ctx)
    bn = _pick(n6, (512, 256, 128))
    return pl.pallas_call(
        _ada_kernel,
        grid=(depth, n6 // bn),
        in_specs=[pl.BlockSpec((SUBLANES, d), lambda l, j: (0, 0)),
                  pl.BlockSpec((1, d, bn), lambda l, j: (l, 0, j)),
                  pl.BlockSpec((1, 1, bn), lambda l, j: (l, 0, j))],
        out_specs=pl.BlockSpec((1, SUBLANES, bn), lambda l, j: (l, 0, j)),
        out_shape=jax.ShapeDtypeStruct((depth, SUBLANES, n6), F32),
        compiler_params=_cparams(("parallel", "parallel"), _vmem([((d, bn), F32), ((d, bn), BF16)])),
        name="ada_mods",
    )(xin, ada_w, ada_b.reshape(depth, 1, n6))


def _normmod_kernel(h_ref, nw_ref, sh_ref, sc_ref, o_ref, *, bm, n_lat):
    x = h_ref[...]
    y = x * lax.rsqrt(jnp.mean(x * x, axis=-1, keepdims=True) + RMS_EPS) * nw_ref[...]
    is_lat = _row_is_lat(pl.program_id(0), bm, n_lat)
    sh = jnp.where(is_lat, sh_ref[0:1, :], sh_ref[1:2, :])
    sc = jnp.where(is_lat, sc_ref[0:1, :], sc_ref[1:2, :])
    o_ref[...] = (y * (1.0 + sc) + sh).astype(BF16)


def _normmod(h, nw, mods, k_shift, k_scale, n_lat):
    t, d = h.shape
    bm = _pick(t, (256, 128))
    return pl.pallas_call(
        functools.partial(_normmod_kernel, bm=bm, n_lat=n_lat),
        grid=(t // bm,),
        in_specs=[pl.BlockSpec((bm, d), lambda i: (i, 0)),
                  pl.BlockSpec((1, d), lambda i: (0, 0)),
                  pl.BlockSpec((SUBLANES, d), lambda i: (0, k_shift)),
                  pl.BlockSpec((SUBLANES, d), lambda i: (0, k_scale))],
        out_specs=pl.BlockSpec((bm, d), lambda i: (i, 0)),
        out_shape=jax.ShapeDtypeStruct((t, d), BF16),
        compiler_params=_cparams(("parallel",), _vmem([((bm, d), F32), ((bm, d), BF16), ((bm, d), F32)])),
        name="norm_modulate",
    )(h, nw.reshape(1, d), mods, mods)


def _final_norm_kernel(h_ref, nw_ref, o_ref):
    x = h_ref[...]
    o_ref[...] = x * lax.rsqrt(jnp.mean(x * x, axis=-1, keepdims=True) + RMS_EPS) * nw_ref[...]


def _final_norm(h, nw, n_lat):
    t, d = h.shape
    bm = _pick(n_lat, (256, 128))
    return pl.pallas_call(
        _final_norm_kernel,
        grid=(n_lat // bm,),
        in_specs=[pl.BlockSpec((bm, d), lambda i: (i, 0)),
                  pl.BlockSpec((1, d), lambda i: (0, 0))],
        out_specs=pl.BlockSpec((bm, d), lambda i: (i, 0)),
        out_shape=jax.ShapeDtypeStruct((n_lat, d), F32),
        compiler_params=_cparams(("parallel",), _vmem([((bm, d), F32), ((bm, d), F32), ((bm, d), F32)])),
        name="final_norm",
    )(h, nw.reshape(1, d))


def _row_block(t, w):
    return _pick(t, (1408, 768, 512, 256, 128) if w.dtype == F32 else (768, 512, 256, 128))


def _col_cands(w):
    return (256, 128) if w.dtype == F32 else (512, 256, 128)


def _wspec(w, layer, bn, ob):
    k = w.shape[-2]
    if w.ndim == 3:
        return pl.BlockSpec((None, k, bn), lambda i, j: (layer, 0, j + ob))
    return pl.BlockSpec((k, bn), lambda i, j: (0, j + ob))


def _mm_plain_kernel(x_ref, w_ref, o_ref):
    o_ref[...] = jnp.dot(x_ref[...], w_ref[...].astype(BF16), preferred_element_type=F32).astype(o_ref.dtype)


def _mm_plain(x, w, out_dtype, *, col_off=0, n_cols=None, layer=0):
    t, k = x.shape
    n_cols = w.shape[-1] - col_off if n_cols is None else n_cols
    bm = _row_block(t, w)
    bn = _pick(math.gcd(n_cols, col_off) if col_off else n_cols, _col_cands(w))
    ob = col_off // bn
    return pl.pallas_call(
        _mm_plain_kernel,
        grid=(t // bm, n_cols // bn),
        in_specs=[pl.BlockSpec((bm, k), lambda i, j: (i, 0)), _wspec(w, layer, bn, ob)],
        out_specs=pl.BlockSpec((bm, bn), lambda i, j: (i, j)),
        out_shape=jax.ShapeDtypeStruct((t, n_cols), out_dtype),
        compiler_params=_cparams(("parallel", "arbitrary"),
                                 _vmem([((bm, k), BF16), ((k, bn), w.dtype), ((bm, bn), F32)])),
        name="matmul",
    )(x, w)


def _qkv_kernel(x_ref, w_ref, cos_ref, s1_ref, s2_ref, o_ref, *, nb_region, bn, q_scale):
    acc = jnp.dot(x_ref[...], w_ref[...].astype(BF16), preferred_element_type=F32)
    region = pl.program_id(1) // nb_region

    @pl.when(region == 1)
    def _():
        o_ref[...] = acc.astype(BF16)

    @pl.when(region != 1)
    def _():
        reps = bn // LANES
        cos = jnp.tile(cos_ref[...], (1, reps))
        s1 = jnp.tile(s1_ref[...], (1, reps))
        s2 = jnp.tile(s2_ref[...], (1, reps))
        out = acc * cos + pltpu.roll(acc, 16, axis=1) * s1 + pltpu.roll(acc, bn - 16, axis=1) * s2
        scale = jnp.where(region == 2, q_scale, 1.0).astype(F32)
        o_ref[...] = (out * scale).astype(BF16)


def _qkv_proj(a, w_in, layer, cos_t, s1_t, s2_t):
    t, k = a.shape
    bm = _row_block(t, w_in)
    bn = _pick(A_WIDTH, _col_cands(w_in))
    n_cols = 3 * A_WIDTH
    return pl.pallas_call(
        functools.partial(_qkv_kernel, nb_region=A_WIDTH // bn, bn=bn, q_scale=A_HEAD_DIM ** -0.5 * math.log2(math.e)),
        grid=(t // bm, n_cols // bn),
        in_specs=[pl.BlockSpec((bm, k), lambda i, j: (i, 0)),
                  _wspec(w_in, layer, bn, 0),
                  pl.BlockSpec((bm, LANES), lambda i, j: (i, 0)),
                  pl.BlockSpec((bm, LANES), lambda i, j: (i, 0)),
                  pl.BlockSpec((bm, LANES), lambda i, j: (i, 0))],
        out_specs=pl.BlockSpec((bm, bn), lambda i, j: (i, j)),
        out_shape=jax.ShapeDtypeStruct((t, n_cols), BF16),
        compiler_params=_cparams(("parallel", "arbitrary"),
                                 _vmem([((bm, k), BF16), ((k, bn), w_in.dtype), ((bm, bn), F32),
                                        ((bm, bn), F32)])),
        name="qkv_rope_proj",
    )(a, w_in, cos_t, s1_t, s2_t)


def _glu_kernel(x_ref, wa_ref, wg_ref, o_ref):
    x = x_ref[...]
    a = jnp.dot(x, wa_ref[...].astype(BF16), preferred_element_type=F32)
    g = jnp.dot(x, wg_ref[...].astype(BF16), preferred_element_type=F32)
    o_ref[...] = (a * _sigmoid(g)).astype(o_ref.dtype)


def _swiglu_up_kernel(x_ref, w1_ref, w3_ref, o_ref):
    x = x_ref[...]
    a = jnp.dot(x, w1_ref[...].astype(BF16), preferred_element_type=F32)
    b = jnp.dot(x, w3_ref[...].astype(BF16), preferred_element_type=F32)
    o_ref[...] = (a * _sigmoid(a) * b).astype(o_ref.dtype)


def _mm_pair(kern, x, wa, wg, out_dtype, *, off_a=0, off_g=0, n_cols=None, layer=0, name="matmul_pair"):
    t, k = x.shape
    n_cols = wa.shape[-1] if n_cols is None else n_cols
    bm = _row_block(t, wa)
    g = n_cols
    for o in (off_a, off_g):
        g = math.gcd(g, o) if o else g
    bn = _pick(g, _col_cands(wa))
    oa, og = off_a // bn, off_g // bn
    return pl.pallas_call(
        kern,
        grid=(t // bm, n_cols // bn),
        in_specs=[pl.BlockSpec((bm, k), lambda i, j: (i, 0)),
                  _wspec(wa, layer, bn, oa),
                  _wspec(wg, layer, bn, og)],
        out_specs=pl.BlockSpec((bm, bn), lambda i, j: (i, j)),
        out_shape=jax.ShapeDtypeStruct((t, n_cols), out_dtype),
        compiler_params=_cparams(("parallel", "arbitrary"),
                                 _vmem([((bm, k), BF16), ((k, bn), wa.dtype), ((k, bn), wg.dtype), ((bm, bn), F32),
                                        ((bm, bn), F32)])),
        name=name,
    )(x, wa, wg)


def _resid1_kernel(x_ref, w_ref, h_ref, gate_ref, o_ref, *, bm, n_lat):
    acc = jnp.dot(x_ref[...], w_ref[...], preferred_element_type=F32)
    is_lat = _row_is_lat(pl.program_id(0), bm, n_lat)
    gate = jnp.where(is_lat, gate_ref[0:1, :], gate_ref[1:2, :])
    o_ref[...] = h_ref[...] + gate * acc


def _resid2_kernel(x1_ref, w1_ref, x2_ref, w2_ref, h_ref, gate_ref, o_ref, *, bm, n_lat):
    acc = jnp.dot(x1_ref[...], w1_ref[...], preferred_element_type=F32)
    acc = acc + jnp.dot(x2_ref[...], w2_ref[...], preferred_element_type=F32)
    is_lat = _row_is_lat(pl.program_id(0), bm, n_lat)
    gate = jnp.where(is_lat, gate_ref[0:1, :], gate_ref[1:2, :])
    o_ref[...] = h_ref[...] + gate * acc


def _mm_resid(xs, ws, h, mods, k_gate, n_lat, *, bn_cands=(512, 256, 128), name="matmul_resid"):
    t, d = h.shape
    bm = _pick(t, (768, 512, 256, 128))
    bn = _pick(d, bn_cands)
    nbd = d // bn
    in_specs, args, blocks = [], [], []
    for x, w in zip(xs, ws):
        k = x.shape[1]
        in_specs += [pl.BlockSpec((bm, k), lambda i, j: (i, 0)), pl.BlockSpec((k, bn), lambda i, j: (0, j))]
        args += [x, w]
        blocks += [((bm, k), BF16), ((k, bn), BF16)]
    in_specs += [pl.BlockSpec((bm, bn), lambda i, j: (i, j)),
                 pl.BlockSpec((SUBLANES, bn), lambda i, j: (0, k_gate * nbd + j))]
    args += [h, mods]
    blocks += [((bm, bn), F32), ((bm, bn), F32), ((bm, bn), F32)]
    kern = _resid1_kernel if len(xs) == 1 else _resid2_kernel
    return pl.pallas_call(
        functools.partial(kern, bm=bm, n_lat=n_lat),
        grid=(t // bm, d // bn),
        in_specs=in_specs,
        out_specs=pl.BlockSpec((bm, bn), lambda i, j: (i, j)),
        out_shape=jax.ShapeDtypeStruct((t, d), F32),
        compiler_params=_cparams(("parallel", "arbitrary"), _vmem(blocks)),
        name=name,
    )(*args)


def _attn_kernel(lam_ref, subw_ref, q_ref, k_ref, v_ref, o_ref, s_scr, p_scr, a_scr, acc_scr, m_scr,
                 *, bq, bk, n_lat, n_ctx, lam_init):
    qi = pl.program_id(1)
    q = q_ref[...]
    lane = lax.broadcasted_iota(jnp.int32, (bq, LANES), 1)
    zero = jnp.zeros_like(q)
    qm = jnp.concatenate([jnp.where(lane < A_HEAD_DIM, q, zero),
                          jnp.where(lane >= A_HEAD_DIM, q, zero)], axis=0)

    n_chunks = n_lat // bk
    rg = 64

    def scores(slot, kc):
        w = kc.shape[0]
        s_scr[slot, :, 0:w] = lax.dot_general(qm, kc, (((1,), (1,)), ((), ())), preferred_element_type=F32)

    def softmax(slot, w):
        for r0 in range(0, 2 * bq, rg):
            s = s_scr[slot, r0:r0 + rg, 0:w]
            m_old = m_scr[r0:r0 + rg, :]
            m_new = jnp.maximum(m_old, jnp.max(s, axis=1, keepdims=True))
            m_scr[r0:r0 + rg, :] = m_new
            a_scr[slot, r0:r0 + rg, :] = jnp.exp2(m_old - m_new)
            p_scr[slot, r0:r0 + rg, 0:w] = jnp.exp2(s - m_new).astype(BF16)

    def accumulate(slot, w, vc):
        vext = jnp.concatenate([vc, jnp.ones_like(vc)], axis=1)
        acc_scr[...] = (a_scr[slot] * acc_scr[...]
                        + jnp.dot(p_scr[slot, :, 0:w], vext, preferred_element_type=F32))

    def kchunk(c):
        return k_ref[pl.ds(pl.multiple_of(c * bk, bk), bk), :]

    def vchunk(c):
        return v_ref[pl.ds(pl.multiple_of(c * bk, bk), bk), :]

    m_scr[...] = jnp.full(m_scr.shape, NEG_BIG, F32)
    acc_scr[...] = jnp.zeros(acc_scr.shape, F32)
    scores(1, k_ref[n_lat:n_lat + n_ctx, :])
    scores(0, k_ref[0:bk, :])
    softmax(1, n_ctx)

    @pl.when(qi >= n_lat // bq)
    def _():
        accumulate(1, n_ctx, v_ref[n_lat:n_lat + n_ctx, :])

    @pl.when(qi < n_lat // bq)
    def _():
        accumulate(1, n_ctx, v_ref[n_lat:n_lat + n_ctx, :])
        softmax(0, bk)
        scores(1, k_ref[bk:2 * bk, :])

        def body(j, carry):
            accumulate(0, bk, vchunk(2 * j))
            softmax(1, bk)
            scores(0, kchunk(2 * j + 2))
            accumulate(1, bk, vchunk(2 * j + 1))
            softmax(0, bk)
            scores(1, kchunk(2 * j + 3))
            return carry

        lax.fori_loop(0, n_chunks // 2 - 1, body, 0)
        accumulate(0, bk, v_ref[(n_chunks - 2) * bk:(n_chunks - 1) * bk, :])
        softmax(1, bk)
        accumulate(1, bk, v_ref[(n_chunks - 1) * bk:n_chunks * bk, :])

    acc = acc_scr[...]
    lam = lam_ref[...]
    lam_full = (jnp.exp(jnp.sum(lam[0:1] * lam[1:2], axis=1, keepdims=True))
                - jnp.exp(jnp.sum(lam[2:3] * lam[3:4], axis=1, keepdims=True)) + lam_init)
    o = acc[:bq, :LANES] / acc[:bq, LANES:] - lam_full * (acc[bq:, :LANES] / acc[bq:, LANES:])
    y = o * lax.rsqrt(jnp.mean(o * o, axis=-1, keepdims=True) + 1e-5) * subw_ref[...]
    o_ref[...] = (y * (1.0 - lam_init)).astype(BF16)


def _attention(kvq, lam, subw, n_lat, n_ctx, lam_init):
    t = kvq.shape[0]
    bq = _pick(math.gcd(n_lat, n_ctx), (256, 128))
    bk = _pick(n_lat // 2, (512, 256, 128))
    nh = A_HEADS
    return pl.pallas_call(
        functools.partial(_attn_kernel, bq=bq, bk=bk, n_lat=n_lat, n_ctx=n_ctx, lam_init=lam_init),
        grid=(nh, t // bq),
        in_specs=[pl.BlockSpec((4, A_HEAD_DIM), lambda h, i: (0, 0)),
                  pl.BlockSpec((1, LANES), lambda h, i: (0, 0)),
                  pl.BlockSpec((bq, LANES), lambda h, i: (i, 2 * nh + h)),
                  pl.BlockSpec((t, LANES), lambda h, i: (0, h)),
                  pl.BlockSpec((t, LANES), lambda h, i: (0, nh + h))],
        out_specs=pl.BlockSpec((bq, LANES), lambda h, i: (i, h)),
        out_shape=jax.ShapeDtypeStruct((t, A_WIDTH), BF16),
        scratch_shapes=[pltpu.VMEM((2, 2 * bq, bk), F32), pltpu.VMEM((2, 2 * bq, bk), BF16),
                        pltpu.VMEM((2, 2 * bq, 1), F32), pltpu.VMEM((2 * bq, 2 * LANES), F32),
                        pltpu.VMEM((2 * bq, 1), F32)],
        compiler_params=_cparams(("parallel", "arbitrary"),
                                 _vmem([((t, LANES), BF16), ((t, LANES), BF16)], scratch=16 << 20)),
        name="diff_attention",
    )(lam, subw.reshape(1, LANES), kvq, kvq, kvq)


def _dwconv_shifts(taps):
    base = HALO - (taps - 1) // 2
    return base, sorted({(base + k) % SUBLANES for k in range(taps)} - {0})


def _dwconv_kernel(prev_ref, cur_ref, next_ref, w_ref, b_ref, lnw_ref, lnb_ref, o_ref, xx_ref, sh_ref, y_ref,
                   *, bm, cb, taps, nb_lat, nb_all, layernorm):
    i = pl.program_id(0)
    has_prev = jnp.logical_and(i != 0, i != nb_lat)
    has_next = jnp.logical_and(i != nb_lat - 1, i != nb_all - 1)
    xx_ref[0:HALO, :] = jnp.where(has_prev, prev_ref[...], 0.0)
    xx_ref[HALO:HALO + bm, :] = cur_ref[...]
    xx_ref[HALO + bm:2 * HALO + bm, :] = jnp.where(has_next, next_ref[...], 0.0)
    base, shifts = _dwconv_shifts(taps)
    nr = bm + 2 * HALO - SUBLANES
    rs = min(bm, 64)
    cs = min(cb, 512)
    for c0 in range(0, cb, cs):
        for n, b in enumerate(shifts):
            sh_ref[n, 0:nr, c0:c0 + cs] = xx_ref[b:b + nr, c0:c0 + cs]
        wts = [w_ref[k:k + 1, c0:c0 + cs] for k in range(taps)]
        bias = b_ref[:, c0:c0 + cs]
        for r0 in range(0, bm, rs):
            acc = jnp.broadcast_to(bias, (rs, cs))
            for k in range(taps):
                b = (base + k) % SUBLANES
                lo = r0 + base + k - b
                if b == 0:
                    win = xx_ref[lo:lo + rs, c0:c0 + cs]
                else:
                    win = sh_ref[shifts.index(b), lo:lo + rs, c0:c0 + cs]
                acc = acc + win * wts[k]
            y_ref[r0:r0 + rs, c0:c0 + cs] = acc
    y = y_ref[...]
    if layernorm:
        mu = jnp.mean(y, axis=-1, keepdims=True)
        var = jnp.mean(jnp.square(y - mu), axis=-1, keepdims=True)
        y = (y - mu) * lax.rsqrt(var + LN_EPS) * lnw_ref[...] + lnb_ref[...]
    o_ref[...] = (y * _sigmoid(y)).astype(o_ref.dtype)


def _dwconv_silu(x, w, b, n_lat, out_dtype, *, col_off=0, n_cols=None, ln=None):
    t = x.shape[0]
    taps = w.shape[0]
    n_cols = x.shape[1] - col_off if n_cols is None else n_cols
    bm = _pick(math.gcd(n_lat, t - n_lat), (256, 128))
    cb = n_cols if ln is not None else _pick(math.gcd(n_cols, col_off) if col_off else n_cols, (1024, 512, 256, 128))
    ob = col_off // cb
    hb = bm // HALO
    n_halo_blocks = t // HALO
    lnw, lnb = ln if ln is not None else (jnp.ones((n_cols,), F32), jnp.zeros((n_cols,), F32))
    n_shifts = len(_dwconv_shifts(taps)[1])
    return pl.pallas_call(
        functools.partial(_dwconv_kernel, bm=bm, cb=cb, taps=taps, nb_lat=n_lat // bm, nb_all=t // bm,
                          layernorm=ln is not None),
        grid=(t // bm, n_cols // cb),
        in_specs=[pl.BlockSpec((HALO, cb), lambda i, j: (jnp.maximum(i * hb - 1, 0), j + ob)),
                  pl.BlockSpec((bm, cb), lambda i, j: (i, j + ob)),
                  pl.BlockSpec((HALO, cb), lambda i, j: (jnp.minimum((i + 1) * hb, n_halo_blocks - 1), j + ob)),
                  pl.BlockSpec((taps, cb), lambda i, j: (0, j)),
                  pl.BlockSpec((1, cb), lambda i, j: (0, j)),
                  pl.BlockSpec((1, cb), lambda i, j: (0, j)),
                  pl.BlockSpec((1, cb), lambda i, j: (0, j))],
        out_specs=pl.BlockSpec((bm, cb), lambda i, j: (i, j)),
        out_shape=jax.ShapeDtypeStruct((t, n_cols), out_dtype),
        scratch_shapes=[pltpu.VMEM((bm + 2 * HALO, cb), F32),
                        pltpu.VMEM((n_shifts, bm + 2 * HALO - SUBLANES, cb), F32),
                        pltpu.VMEM((bm, cb), F32)],
        compiler_params=_cparams(("parallel", "parallel"),
                                 _vmem([((bm, cb), F32), ((bm, cb), F32)],
                                       scratch=(n_shifts + 2) * (bm + 2 * HALO) * cb * 4)),
        name="dwconv_silu",
    )(x, x, x, w, b.reshape(1, n_cols), lnw.reshape(1, n_cols), lnb.reshape(1, n_cols))


def _s5_build_kernel(u_ref, lam_ref, step_ref, bt_ref, c_ref, yz_ref, et_ref, adv_ref):
    L, H, P = S5_CHUNK, S5_GROUP, S5_STATE
    lane = lax.broadcasted_iota(jnp.int32, (1, 2 * P), 1)
    first = lane < P
    sgn = jnp.where(first, -1.0, 1.0).astype(F32)
    conj_sgn = -sgn

    def swap(x):
        return pltpu.roll(x, P, axis=1)

    def dupr(x):
        return jnp.where(first, x, swap(x))

    def dupi(x):
        return jnp.where(first, swap(x), x)

    def cmul(a, b):
        return dupr(a) * b + (sgn * dupi(a)) * swap(b)

    one = jnp.where(first, 1.0, 0.0).astype(F32)
    mats, ets, advs = [], [], []
    for d in range(2):
        lam = lam_ref[0, d:d + 1, :]
        z = lam * jnp.exp(step_ref[0, d:d + 1, :])
        ang = dupi(z)
        lbar = jnp.exp(dupr(z)) * jnp.where(first, jnp.cos(ang), jnp.sin(ang))
        den = dupr(lam) * dupr(lam) + dupi(lam) * dupi(lam)
        qcoef = cmul(lbar - one, lam * conj_sgn) / den
        bbar = cmul(qcoef, bt_ref[0])
        cmat = c_ref[0, d * H:(d + 1) * H, :]
        pw = [one]
        for _ in range(L):
            pw.append(cmul(pw[-1], lbar))
        cpow = [cmul(pw[t], cmat) * conj_sgn for t in range(L + 1)]
        if d == 0:
            kern_rows = jnp.concatenate([cpow[t] for t in range(L)], axis=0)
            inj = jnp.concatenate([cmul(pw[L - 1 - j], bbar) for j in range(L)], axis=0)
            read = jnp.concatenate([cpow[t + 1] for t in range(L)], axis=0)
        else:
            kern_rows = jnp.concatenate([cpow[L - 1 - t] for t in range(L)], axis=0)
            inj = jnp.concatenate([cmul(pw[j], bbar) for j in range(L)], axis=0)
            read = jnp.concatenate([cpow[L - t] for t in range(L)], axis=0)
        base = lax.dot_general(bbar.astype(BF16), kern_rows.astype(BF16), (((1,), (1,)), ((), ())),
                               preferred_element_type=F32)
        lane_o = lax.broadcasted_iota(jnp.int32, (H, L * H), 1)
        blocks = []
        for j in range(L):
            if d == 0:
                sh = (j * H) % (L * H)
                blk = jnp.where(lane_o >= j * H, pltpu.roll(base, sh, axis=1) if sh else base, 0.0)
            else:
                sh = (L * H - (L - 1 - j) * H) % (L * H)
                blk = jnp.where(lane_o < (j + 1) * H, pltpu.roll(base, sh, axis=1) if sh else base, 0.0)
            blocks.append(blk)
        mats.append((jnp.concatenate(blocks, axis=0), inj))
        ets.append(read)
        pl_ = pw[L]
        advs.append((dupr(pl_), sgn * dupi(pl_)))
    mix = (mats[0][0] + mats[1][0]).astype(BF16)
    rhs = jnp.concatenate([mix, mats[0][1].astype(BF16), mats[1][1].astype(BF16)], axis=1)
    yz_ref[0] = jnp.dot(u_ref[0], rhs, preferred_element_type=F32)
    et_ref[0] = jnp.concatenate([ets[0], ets[1]], axis=1).astype(BF16)
    adv_ref[0] = jnp.concatenate([jnp.concatenate([advs[0][0], advs[1][0]], axis=1),
                                  jnp.concatenate([advs[0][1], advs[1][1]], axis=1)], axis=0)


def _s5_scan_kernel(z_ref, adv_ref, s_ref, *, n_chunks, nc_ctx):
    P = S5_STATE
    a1 = adv_ref[:, 0, :]
    a2 = adv_ref[:, 1, :]
    gb = a1.shape[0]
    nl = n_chunks - nc_ctx

    def swap_halves(x):
        return jnp.concatenate([pltpu.roll(x[:, :2 * P], P, axis=1), pltpu.roll(x[:, 2 * P:], P, axis=1)], axis=1)

    def body(i, carry):
        s, ssw = carry
        cf = jnp.where(i < nc_ctx, nl + i, i - nc_ctx)
        cb = n_chunks - 1 - i
        z = jnp.concatenate([z_ref[cf][:, :2 * P], z_ref[cb][:, 2 * P:]], axis=1)
        s_ref[cf, :, 0:2 * P] = s[:, :2 * P].astype(s_ref.dtype)
        s_ref[cb, :, 2 * P:4 * P] = s[:, 2 * P:].astype(s_ref.dtype)
        return a1 * s + a2 * ssw + z, a1 * ssw - a2 * s + swap_halves(z)

    zero = jnp.zeros((gb, 4 * P), F32)
    lax.fori_loop(0, n_chunks, body, (zero, zero), unroll=4)


def _s5_out_kernel(yz_ref, s_ref, et_ref, o_ref):
    n = o_ref.shape[2]
    o_ref[0] = yz_ref[0, :, 0:n] + lax.dot_general(s_ref[0], et_ref[0], (((1,), (1,)), ((), ())),
                                                  preferred_element_type=F32)


def _s5_finish_kernel(y_ref, u_ref, d_ref, w_ref, b_ref, o_ref):
    y = y_ref[...] + d_ref[...] * u_ref[...]
    g = 0.5 * y * (1.0 + jnp.tanh(math.sqrt(2.0 / math.pi) * (y + 0.044715 * (y * y * y))))
    gate = jnp.dot(g.astype(BF16), w_ref[...], preferred_element_type=F32) + b_ref[...]
    o_ref[...] = (g * _sigmoid(gate)).astype(BF16)


def _s5_branch(u, lam_re, lam_im, log_step, b_re, b_im, c_re, c_im, d_skip, glu_w, glu_b, n_lat):
    t = u.shape[0]
    G, H, P, L = S5_GROUPS, S5_GROUP, S5_STATE, S5_CHUNK
    nch = t // L
    nc_ctx = (t - n_lat) // L
    lam_p = jnp.concatenate([lam_re, lam_im], axis=-1).transpose(1, 0, 2)
    step_p = jnp.broadcast_to(log_step.T[:, :, None], (G, 2, 2 * P))
    bt_p = jnp.concatenate([b_re, b_im], axis=1).transpose(0, 2, 1)
    c_p = jnp.concatenate([c_re, c_im], axis=-1).transpose(1, 0, 2, 3).reshape(G, 2 * H, 2 * P)
    ug = u.astype(BF16).reshape(nch, L, G, H).transpose(2, 0, 1, 3).reshape(G, nch, L * H)
    LH = L * H
    yz, et, adv = pl.pallas_call(
        _s5_build_kernel,
        grid=(G,),
        in_specs=[pl.BlockSpec((1, nch, LH), lambda g: (g, 0, 0)),
                  pl.BlockSpec((1, 2, 2 * P), lambda g: (g, 0, 0)),
                  pl.BlockSpec((1, 2, 2 * P), lambda g: (g, 0, 0)),
                  pl.BlockSpec((1, H, 2 * P), lambda g: (g, 0, 0)),
                  pl.BlockSpec((1, 2 * H, 2 * P), lambda g: (g, 0, 0))],
        out_specs=[pl.BlockSpec((1, nch, LH + 4 * P), lambda g: (g, 0, 0)),
                   pl.BlockSpec((1, LH, 4 * P), lambda g: (g, 0, 0)),
                   pl.BlockSpec((1, 2, 4 * P), lambda g: (g, 0, 0))],
        out_shape=[jax.ShapeDtypeStruct((G, nch, LH + 4 * P), F32),
                   jax.ShapeDtypeStruct((G, LH, 4 * P), BF16),
                   jax.ShapeDtypeStruct((G, 2, 4 * P), F32)],
        compiler_params=_cparams(("parallel",), _vmem([((nch, LH), BF16), ((nch, LH + 4 * P), F32)])),
        name="s5_build_apply",
    )(ug, lam_p, step_p, bt_p, c_p)
    z_t = yz[:, :, LH:].transpose(1, 0, 2)
    gb = _pick(G, (16, 8))
    s_in = pl.pallas_call(
        functools.partial(_s5_scan_kernel, n_chunks=nch, nc_ctx=nc_ctx),
        grid=(G // gb,),
        in_specs=[pl.BlockSpec((nch, gb, 4 * P), lambda g: (0, g, 0)),
                  pl.BlockSpec((gb, 2, 4 * P), lambda g: (g, 0, 0))],
        out_specs=pl.BlockSpec((nch, gb, 4 * P), lambda g: (0, g, 0)),
        out_shape=jax.ShapeDtypeStruct((nch, G, 4 * P), BF16),
        compiler_params=_cparams(("parallel",), _vmem([((nch, gb, 4 * P), F32), ((nch, gb, 4 * P), BF16)])),
        name="s5_chunk_scan",
    )(z_t, adv)
    s_g = s_in.transpose(1, 0, 2)
    y = pl.pallas_call(
        _s5_out_kernel,
        grid=(G,),
        in_specs=[pl.BlockSpec((1, nch, LH + 4 * P), lambda g: (g, 0, 0)),
                  pl.BlockSpec((1, nch, 4 * P), lambda g: (g, 0, 0)),
                  pl.BlockSpec((1, LH, 4 * P), lambda g: (g, 0, 0))],
        out_specs=pl.BlockSpec((1, nch, LH), lambda g: (g, 0, 0)),
        out_shape=jax.ShapeDtypeStruct((G, nch, LH), F32),
        compiler_params=_cparams(("parallel",), _vmem([((nch, LH + 4 * P), F32), ((nch, 4 * P), BF16),
                                                       ((nch, LH), F32)])),
        name="s5_readout",
    )(yz, s_g, et)
    y_tok = y.reshape(G, nch, L, H).transpose(1, 2, 0, 3).reshape(t, G * H)
    w = G * H
    bm = _pick(t, (256, 128))
    return pl.pallas_call(
        _s5_finish_kernel,
        grid=(t // bm,),
        in_specs=[pl.BlockSpec((bm, w), lambda i: (i, 0)),
                  pl.BlockSpec((bm, w), lambda i: (i, 0)),
                  pl.BlockSpec((1, w), lambda i: (0, 0)),
                  pl.BlockSpec((w, w), lambda i: (0, 0)),
                  pl.BlockSpec((1, w), lambda i: (0, 0))],
        out_specs=pl.BlockSpec((bm, w), lambda i: (i, 0)),
        out_shape=jax.ShapeDtypeStruct((t, w), BF16),
        compiler_params=_cparams(("parallel",), _vmem([((bm, w), F32), ((bm, w), F32), ((w, w), BF16),
                                                       ((bm, w), F32)])),
        name="s5_gelu_glu",
    )(y_tok, u, d_skip.reshape(1, w), glu_w.astype(BF16), glu_b.reshape(1, w))


def _ssd_chunk(forward, xs_ref, b_ref, c_ref, dt_ref, sel_ref, bias_ref, alog_ref, y_ref, state_ref, hpg):
    L = SSD_CHUNK
    hd = M2_HEAD_DIM
    sel = sel_ref[0]
    hi, mid, lo = _split3(dt_ref[...])
    dt_raw = (jnp.dot(hi, sel, preferred_element_type=F32) + jnp.dot(mid, sel, preferred_element_type=F32)
              + jnp.dot(lo, sel, preferred_element_type=F32))
    v = dt_raw + bias_ref[0]
    dt = jnp.maximum(v, 0.0) + jnp.log1p(jnp.exp(-jnp.abs(v)))
    da = dt * (-jnp.exp(alog_ref[0]))
    ti = lax.broadcasted_iota(jnp.int32, (L, L), 0)
    si = lax.broadcasted_iota(jnp.int32, (L, L), 1)
    tri = jnp.where(si <= ti, 1.0, 0.0).astype(BF16)
    hi, mid, lo = _split3(da)
    cum = (jnp.dot(tri, hi, preferred_element_type=F32) + jnp.dot(tri, mid, preferred_element_type=F32)
           + jnp.dot(tri, lo, preferred_element_type=F32))
    total = cum[L - 1:L, :]
    if forward:
        cumq = cum
        w_out = jnp.exp(cumq)
        w_state = jnp.exp(total - cumq)
        mask = si <= ti
    else:
        cumq = cum - da
        w_out = jnp.exp(total - cumq)
        w_state = jnp.exp(cumq)
        mask = si >= ti
    xsdt = xs_ref[...] * dt
    cb_ = c_ref[...]
    bb_ = b_ref[...]
    s_old = state_ref[...]
    y_off = w_out * jnp.dot(cb_, s_old.astype(BF16), preferred_element_type=F32)
    state_ref[...] = jnp.exp(total) * s_old + lax.dot_general(
        bb_, (w_state * xsdt).astype(BF16), (((0,), (0,)), ((), ())), preferred_element_type=F32)
    cbm = lax.dot_general(cb_, bb_, (((1,), (1,)), ((), ())), preferred_element_type=F32)
    cum_t = cumq.T
    lane = lax.broadcasted_iota(jnp.int32, (L, LANES), 1)
    xb = xsdt.astype(BF16)
    zero = jnp.zeros((L, LANES), BF16)
    outs = []
    for pr in range(hpg // 2):
        gs = []
        for r in (2 * pr, 2 * pr + 1):
            col = cumq[:, hd * r:hd * r + 1]
            row = cum_t[hd * r:hd * r + 1, :]
            diff = (col - row) if forward else (row - col)
            decay = jnp.exp(jnp.where(mask, diff, NEG_BIG))
            gs.append((cbm * decay).astype(BF16))
        xp = xb[:, LANES * pr:LANES * (pr + 1)]
        rhs = jnp.concatenate([jnp.where(lane < hd, xp, zero), jnp.where(lane >= hd, xp, zero)], axis=0)
        outs.append(jnp.dot(jnp.concatenate(gs, axis=1), rhs, preferred_element_type=F32))
    y_ref[...] = jnp.concatenate(outs, axis=1) + y_off


def _ssd_kernel(*refs, hpg):
    fwd, bwd, (yf_ref, yb_ref, sf_ref, sb_ref) = refs[0:7], refs[7:14], refs[14:18]

    @pl.when(pl.program_id(1) == 0)
    def _():
        sf_ref[...] = jnp.zeros_like(sf_ref)
        sb_ref[...] = jnp.zeros_like(sb_ref)

    _ssd_chunk(True, *fwd, yf_ref, sf_ref, hpg)
    _ssd_chunk(False, *bwd, yb_ref, sb_ref, hpg)


def _ssd_finish_kernel(yf_ref, yb_ref, xs_ref, z_ref, d_ref, nw_ref, o_ref, *, n_groups):
    y = yf_ref[...] + yb_ref[...] + d_ref[...] * xs_ref[...]
    z = z_ref[...]
    y = y * (z * _sigmoid(z))
    gw = y.shape[1] // n_groups
    parts = []
    for g in range(n_groups):
        seg = y[:, g * gw:(g + 1) * gw]
        parts.append(seg * lax.rsqrt(jnp.mean(seg * seg, axis=-1, keepdims=True) + RMS_EPS))
    o_ref[...] = (jnp.concatenate(parts, axis=1) * nw_ref[...]).astype(BF16)


def _ssd_branch(xs, bm_, cm_, dt_raw, z, a_log, dt_bias, d_skip, norm_w, n_lat):
    t, inner = xs.shape
    L = SSD_CHUNK
    ng = M2_GROUPS
    hpg = M2_HEADS // ng
    w = hpg * M2_HEAD_DIM
    nch = t // L
    nc_ctx = (t - n_lat) // L
    nl = nch - nc_ctx
    src = (jnp.arange(2)[:, None, None] * M2_HEADS + jnp.arange(ng)[None, :, None] * hpg
           + jnp.arange(w)[None, None, :] // M2_HEAD_DIM)
    sel = (jnp.arange(LANES)[None, None, :, None] == src[:, :, None, :]).astype(BF16)
    expand = lambda p: jnp.repeat(p, M2_HEAD_DIM, axis=-1).reshape(2, ng, 1, w)
    bias_e = expand(dt_bias)
    alog_e = expand(a_log)

    order = (lambda c: jnp.where(c < nc_ctx, nl + c, c - nc_ctx), lambda c: nch - 1 - c)
    in_specs, args = [], []
    for d in range(2):
        ch = order[d]
        in_specs += [pl.BlockSpec((L, w), lambda g, c, ch=ch: (ch(c), g)),
                     pl.BlockSpec((L, M2_STATE), lambda g, c, ch=ch: (ch(c), g)),
                     pl.BlockSpec((L, M2_STATE), lambda g, c, ch=ch: (ch(c), g)),
                     pl.BlockSpec((L, LANES), lambda g, c, ch=ch: (ch(c), 0)),
                     pl.BlockSpec((1, LANES, w), lambda g, c: (g, 0, 0)),
                     pl.BlockSpec((1, 1, w), lambda g, c: (g, 0, 0)),
                     pl.BlockSpec((1, 1, w), lambda g, c: (g, 0, 0))]
        args += [xs, bm_, cm_, dt_raw, sel[d], bias_e[d], alog_e[d]]
    yf, yb = pl.pallas_call(
        functools.partial(_ssd_kernel, hpg=hpg),
        grid=(ng, nch),
        in_specs=in_specs,
        out_specs=[pl.BlockSpec((L, w), lambda g, c: (order[0](c), g)),
                   pl.BlockSpec((L, w), lambda g, c: (order[1](c), g))],
        out_shape=[jax.ShapeDtypeStruct((t, inner), F32), jax.ShapeDtypeStruct((t, inner), F32)],
        scratch_shapes=[pltpu.VMEM((M2_STATE, w), F32), pltpu.VMEM((M2_STATE, w), F32)],
        compiler_params=_cparams(("parallel", "arbitrary"), 32 << 20),
        name="ssd_chunk_scan",
    )(*args)
    bm = _pick(t, (256, 128))
    d_e = jnp.repeat(d_skip, M2_HEAD_DIM).reshape(1, inner)
    return pl.pallas_call(
        functools.partial(_ssd_finish_kernel, n_groups=ng),
        grid=(t // bm,),
        in_specs=[pl.BlockSpec((bm, inner), lambda i: (i, 0)),
                  pl.BlockSpec((bm, inner), lambda i: (i, 0)),
                  pl.BlockSpec((bm, inner), lambda i: (i, 0)),
                  pl.BlockSpec((bm, inner), lambda i: (i, 0)),
                  pl.BlockSpec((1, inner), lambda i: (0, 0)),
                  pl.BlockSpec((1, inner), lambda i: (0, 0))],
        out_specs=pl.BlockSpec((bm, inner), lambda i: (i, 0)),
        out_shape=jax.ShapeDtypeStruct((t, inner), BF16),
        compiler_params=_cparams(("parallel",), _vmem([((bm, inner), F32)] * 5)),
        name="ssd_gate_norm",
    )(yf, yb, xs, z, d_e, norm_w.reshape(1, inner))


def _rope_tables(n_lat, n_ctx):
    n_rows = n_lat // GRID_W
    rows = jnp.repeat(jnp.arange(n_rows), GRID_W).astype(F32)
    cols = jnp.tile(jnp.arange(GRID_W), n_rows).astype(F32)
    inv = jnp.power(ROPE_BASE, -jnp.arange(0, ROPE_AXIS_DIM, 2, dtype=F32) / ROPE_AXIS_DIM)
    ang_r = rows[:, None] * inv
    ang_c = cols[:, None] * inv
    ang = jnp.concatenate([ang_r, ang_r, ang_c, ang_c], axis=-1)
    cos, sin = jnp.cos(ang), jnp.sin(ang)
    half = ROPE_AXIS_DIM // 2
    first = (jnp.arange(A_HEAD_DIM) % ROPE_AXIS_DIM) < half
    s1 = jnp.where(first, 0.0, sin)
    s2 = jnp.where(first, -sin, 0.0)
    pad = lambda tbl, fill: jnp.concatenate([tbl, jnp.full((n_ctx, A_HEAD_DIM), fill, F32)], axis=0)
    two = lambda tbl: jnp.concatenate([tbl, tbl], axis=-1)
    return two(pad(cos, 1.0)), two(pad(s1, 0.0)), two(pad(s2, 0.0))


def _swiglu_ffn(h, mods, nw, w1_all, w3_all, layer, w2, n_lat):
    a = _normmod(h, nw, mods, 3, 4, n_lat)
    hid = _mm_pair(_swiglu_up_kernel, a, w1_all, w3_all, BF16, layer=layer, name="ffn_gate_up")
    return _mm_resid([hid], [w2], h, mods, 5, n_lat, bn_cands=(256, 128), name="ffn_down_resid")


def _even_layer(h, mods, nw, w_in_all, j, w_out, lam, subw, conv_w, conv_b, ln_w, ln_b, rope, layer_idx, n_lat):
    n_ctx = h.shape[0] - n_lat
    a = _normmod(h, nw, mods, 0, 1, n_lat)
    kvq = _qkv_proj(a, w_in_all, j, *rope)
    glu = _mm_pair(_glu_kernel, a, w_in_all, w_in_all, F32, off_a=EV_B0, off_g=EV_B0 + B_WIDTH, n_cols=B_WIDTH,
                   layer=j, name="conv_glu_proj")
    lam_init = 0.8 - 0.6 * math.exp(-0.3 * layer_idx)
    att = _attention(kvq, lam, subw, n_lat, n_ctx, lam_init)
    cv = _dwconv_silu(glu, conv_w, conv_b, n_lat, BF16, ln=(ln_w, ln_b))
    w_out = w_out.astype(BF16)
    return _mm_resid([att, cv], [w_out[:A_WIDTH], w_out[A_WIDTH:]], h, mods, 2, n_lat, name="even_out_resid")


def _odd_layer(h, mods, nw, w_in_all, j, w_out, s5p, m2p, n_lat):
    a = _normmod(h, nw, mods, 0, 1, n_lat)
    w_in = w_in_all[j]
    u = _mm_plain(a, w_in_all, F32, col_off=0, n_cols=S5_WIDTH, layer=j)
    xbc = _mm_plain(a, w_in_all, F32, col_off=S5_WIDTH, n_cols=M2_CONV_DIM, layer=j)
    n_dt = 2 * M2_HEADS
    w_dt = jnp.pad(w_in[:, OD_DT0:OD_Z0], ((0, 0), (0, LANES - n_dt))).astype(BF16)
    dt_raw = _mm_plain(a, w_dt, F32)
    z = _mm_plain(a, w_in[:, OD_Z0:].astype(BF16), F32)
    s5_out = _s5_branch(u, *s5p, n_lat)
    conv_w, conv_b, a_log, dt_bias, m2_d, m2_norm_w = m2p
    gn = M2_GROUPS * M2_STATE
    xs = _dwconv_silu(xbc, conv_w[:, :M2_INNER], conv_b[:M2_INNER], n_lat, F32, col_off=0, n_cols=M2_INNER)
    bmat = _dwconv_silu(xbc, conv_w[:, M2_INNER:M2_INNER + gn], conv_b[M2_INNER:M2_INNER + gn], n_lat, BF16,
                        col_off=M2_INNER, n_cols=gn)
    cmat = _dwconv_silu(xbc, conv_w[:, M2_INNER + gn:], conv_b[M2_INNER + gn:], n_lat, BF16,
                        col_off=M2_INNER + gn, n_cols=gn)
    ssd_out = _ssd_branch(xs, bmat, cmat, dt_raw, z, a_log, dt_bias, m2_d, m2_norm_w, n_lat)
    w_out = w_out.astype(BF16)
    return _mm_resid([s5_out, ssd_out], [w_out[:S5_WIDTH], w_out[S5_WIDTH:]], h, mods, 2, n_lat,
                     name="odd_out_resid")


def kernel(x, c, ctx, c_ctx, ada_w, ada_b, norm_w, ffn_w1, ffn_w3, ffn_w2, ev_w_in, ev_w_out, ev_lambda, ev_subln_w, ev_conv_w, ev_conv_b, ev_ln_w, ev_ln_b, od_w_in, od_w_out, s5_lam_re, s5_lam_im, s5_log_step, s5_b_re, s5_b_im, s5_c_re, s5_c_im, s5_d, s5_glu_w, s5_glu_b, m2_conv_w, m2_conv_b, m2_a_log, m2_dt_bias, m2_d, m2_norm_w, final_norm_w):
    assert x.shape[0] == 1 and c.shape[0] == 1 and ctx.shape[0] == 1
    n_lat, n_ctx = x.shape[1], ctx.shape[1]
    depth = ada_w.shape[0]
    h = jnp.concatenate([x[0], ctx[0]], axis=0)
    mods_all = _ada_mods(c, c_ctx, ada_w, ada_b)
    rope = _rope_tables(n_lat, n_ctx)
    for i in range(depth):
        mods = mods_all[i]
        j = i // 2
        if i % 2 == 0:
            h = _even_layer(h, mods, norm_w[i, 0], ev_w_in, j, ev_w_out[j], ev_lambda[j], ev_subln_w[j],
                            ev_conv_w[j], ev_conv_b[j], ev_ln_w[j], ev_ln_b[j], rope, i, n_lat)
        else:
            s5p = (s5_lam_re[j], s5_lam_im[j], s5_log_step[j], s5_b_re[j], s5_b_im[j], s5_c_re[j], s5_c_im[j],
                   s5_d[j], s5_glu_w[j], s5_glu_b[j])
            m2p = (m2_conv_w[j], m2_conv_b[j], m2_a_log[j], m2_dt_bias[j], m2_d[j], m2_norm_w[j])
            h = _odd_layer(h, mods, norm_w[i, 0], od_w_in, j, od_w_out[j], s5p, m2p, n_lat)
        h = _swiglu_ffn(h, mods, norm_w[i, 1], ffn_w1, ffn_w3, i, ffn_w2[i].astype(BF16), n_lat)
    return _final_norm(h, final_norm_w, n_lat)[None]
```

```python
import functools
import math

import jax
import jax.numpy as jnp
from jax import lax
from jax.experimental import pallas as pl
from jax.experimental.pallas import tpu as pltpu

F32 = jnp.float32
BF16 = jnp.bfloat16

D_MODEL = 4096
SEQ = 8192
DEPTH = 2
CTX_LEN = 256
GRID_W = 64
RMS_EPS = 1e-6
LN_EPS = 1e-5
A_WIDTH = D_MODEL // 2
A_HEAD_DIM = 64
A_HEADS = A_WIDTH // (2 * A_HEAD_DIM)
B_WIDTH = D_MODEL - A_WIDTH
CONV_WIDTH = 31
ROPE_BASE = 10000.0
ROPE_AXIS_DIM = A_HEAD_DIM // 2
EV_V0 = A_WIDTH
EV_Q0 = 2 * A_WIDTH
EV_B0 = 3 * A_WIDTH
S5_WIDTH = D_MODEL // 4
S5_GROUP = 16
S5_GROUPS = S5_WIDTH // S5_GROUP
S5_STATE = 64
M2_INNER = D_MODEL - S5_WIDTH
M2_HEAD_DIM = 64
M2_HEADS = M2_INNER // M2_HEAD_DIM
M2_GROUPS = 8
M2_STATE = 128
M2_CONV = 5
M2_CONV_DIM = M2_INNER + 2 * M2_GROUPS * M2_STATE
OD_DT0 = S5_WIDTH + M2_CONV_DIM
OD_Z0 = OD_DT0 + 2 * M2_HEADS

V7X_VMEM_BYTES = 64 * 1024 * 1024
LANES = 128
SUBLANES = 8
HALO = 16
S5_CHUNK = 16
SSD_CHUNK = 128
NEG_BIG = -1e30


def _cparams(sem, vmem_bytes):
    return pltpu.CompilerParams(dimension_semantics=sem,
                                vmem_limit_bytes=int(min(vmem_bytes, V7X_VMEM_BYTES - (4 << 20))))


def _pick(n, cands):
    for c in cands:
        if n % c == 0:
            return c
    raise ValueError(f"no block size in {cands} divides {n}")


def _nbytes(shape, dtype):
    return math.prod(shape) * jnp.dtype(dtype).itemsize


def _vmem(blocks, scratch=0):
    return 2 * sum(_nbytes(s, d) for s, d in blocks) + scratch + (12 << 20)


def _sigmoid(x):
    return jax.nn.sigmoid(x)


def _split3(x):
    hi = x.astype(BF16)
    r1 = x - hi.astype(F32)
    mid = r1.astype(BF16)
    lo = (r1 - mid.astype(F32)).astype(BF16)
    return hi, mid, lo


def _row_is_lat(i, bm, n_lat):
    row = i * bm + lax.broadcasted_iota(jnp.int32, (bm, 1), 0)
    return row < n_lat


def _ada_kernel(x_ref, w_ref, b_ref, o_ref):
    x = x_ref[...]
    xs = (x * _sigmoid(x)).astype(BF16)
    o_ref[0] = jnp.dot(xs, w_ref[0].astype(BF16), preferred_element_type=F32) + b_ref[0]


def _ada_mods(c, c_ctx, ada_w, ada_b):
    depth, d, n6 = ada_w.shape
    xin = jnp.zeros((SUBLANES, d), F32).at[0].set(c[0]).at[1].set(c_ctx)
    bn = _pick(n6, (512, 256, 128))
    return pl.pallas_call(
        _ada_kernel,
        grid=(depth, n6 // bn),
        in_specs=[pl.BlockSpec((SUBLANES, d), lambda l, j: (0, 0)),
                  pl.BlockSpec((1, d, bn), lambda l, j: (l, 0, j)),
                  pl.BlockSpec((1, 1, bn), lambda l, j: (l, 0, j))],
        out_specs=pl.BlockSpec((1, SUBLANES, bn), lambda l, j: (l, 0, j)),
        out_shape=jax.ShapeDtypeStruct((depth, SUBLANES, n6), F32),
        compiler_params=_cparams(("parallel", "parallel"), _vmem([((d, bn), F32), ((d, bn), BF16)])),
        name="ada_mods",
    )(xin, ada_w, ada_b.reshape(depth, 1, n6))


def _normmod_kernel(h_ref, nw_ref, sh_ref, sc_ref, o_ref, *, bm, n_lat):
    x = h_ref[...]
    y = x * lax.rsqrt(jnp.mean(x * x, axis=-1, keepdims=True) + RMS_EPS) * nw_ref[...]
    is_lat = _row_is_lat(pl.program_id(0), bm, n_lat)
    sh = jnp.where(is_lat, sh_ref[0:1, :], sh_ref[1:2, :])
    sc = jnp.where(is_lat, sc_ref[0:1, :], sc_ref[1:2, :])
    o_ref[...] = (y * (1.0 + sc) + sh).astype(BF16)


def _normmod(h, nw, mods, k_shift, k_scale, n_lat):
    t, d = h.shape
    bm = _pick(t, (384, 256, 128))
    return pl.pallas_call(
        functools.partial(_normmod_kernel, bm=bm, n_lat=n_lat),
        grid=(t // bm,),
        in_specs=[pl.BlockSpec((bm, d), lambda i: (i, 0)),
                  pl.BlockSpec((1, d), lambda i: (0, 0)),
                  pl.BlockSpec((SUBLANES, d), lambda i: (0, k_shift)),
                  pl.BlockSpec((SUBLANES, d), lambda i: (0, k_scale))],
        out_specs=pl.BlockSpec((bm, d), lambda i: (i, 0)),
        out_shape=jax.ShapeDtypeStruct((t, d), BF16),
        compiler_params=_cparams(("parallel",), _vmem([((bm, d), F32), ((bm, d), BF16), ((bm, d), F32)])),
        name="norm_modulate",
    )(h, nw.reshape(1, d), mods, mods)


def _normmod_first_kernel(x_ref, ctx_ref, nw_ref, sh_ref, sc_ref, o_ref, h_ref, *, nb_lat):
    is_lat = pl.program_id(0) < nb_lat
    x = jnp.where(is_lat, x_ref[...], ctx_ref[...])
    h_ref[...] = x
    y = x * lax.rsqrt(jnp.mean(x * x, axis=-1, keepdims=True) + RMS_EPS) * nw_ref[...]
    sh = jnp.where(is_lat, sh_ref[0:1, :], sh_ref[1:2, :])
    sc = jnp.where(is_lat, sc_ref[0:1, :], sc_ref[1:2, :])
    o_ref[...] = (y * (1.0 + sc) + sh).astype(BF16)


def _normmod_first(x, ctx, nw, mods, k_shift, k_scale):
    (n_lat, d), n_ctx = x.shape, ctx.shape[0]
    t = n_lat + n_ctx
    bm = _pick(math.gcd(n_lat, n_ctx), (256, 128))
    nb_lat = n_lat // bm
    return pl.pallas_call(
        functools.partial(_normmod_first_kernel, nb_lat=nb_lat),
        grid=(t // bm,),
        in_specs=[pl.BlockSpec((bm, d), lambda i: (jnp.minimum(i, nb_lat - 1), 0)),
                  pl.BlockSpec((bm, d), lambda i: (jnp.maximum(i - nb_lat, 0), 0)),
                  pl.BlockSpec((1, d), lambda i: (0, 0)),
                  pl.BlockSpec((SUBLANES, d), lambda i: (0, k_shift)),
                  pl.BlockSpec((SUBLANES, d), lambda i: (0, k_scale))],
        out_specs=[pl.BlockSpec((bm, d), lambda i: (i, 0)), pl.BlockSpec((bm, d), lambda i: (i, 0))],
        out_shape=[jax.ShapeDtypeStruct((t, d), BF16), jax.ShapeDtypeStruct((t, d), F32)],
        compiler_params=_cparams(("parallel",), _vmem([((bm, d), F32)] * 4)),
        name="norm_modulate_first",
    )(x, ctx, nw.reshape(1, d), mods, mods)


def _final_norm_kernel(h_ref, nw_ref, o_ref):
    x = h_ref[...]
    o_ref[...] = x * lax.rsqrt(jnp.mean(x * x, axis=-1, keepdims=True) + RMS_EPS) * nw_ref[...]


def _final_norm(h, nw, n_lat):
    t, d = h.shape
    bm = _pick(n_lat, (256, 128))
    return pl.pallas_call(
        _final_norm_kernel,
        grid=(n_lat // bm,),
        in_specs=[pl.BlockSpec((bm, d), lambda i: (i, 0)),
                  pl.BlockSpec((1, d), lambda i: (0, 0))],
        out_specs=pl.BlockSpec((bm, d), lambda i: (i, 0)),
        out_shape=jax.ShapeDtypeStruct((n_lat, d), F32),
        compiler_params=_cparams(("parallel",), _vmem([((bm, d), F32), ((bm, d), F32), ((bm, d), F32)])),
        name="final_norm",
    )(h, nw.reshape(1, d))


def _row_block(t, w):
    return _pick(t, (1408, 768, 512, 256, 128) if w.dtype == F32 else (768, 512, 256, 128))


def _col_cands(w):
    return (256, 128) if w.dtype == F32 else (512, 256, 128)


def _wspec(w, layer, bn, ob):
    k = w.shape[-2]
    if w.ndim == 3:
        return pl.BlockSpec((None, k, bn), lambda i, j: (layer, 0, j + ob))
    return pl.BlockSpec((k, bn), lambda i, j: (0, j + ob))


def _mm_plain_kernel(x_ref, w_ref, o_ref):
    o_ref[...] = jnp.dot(x_ref[...], w_ref[...].astype(BF16), preferred_element_type=F32).astype(o_ref.dtype)


def _mm_plain(x, w, out_dtype, *, col_off=0, n_cols=None, layer=0):
    t, k = x.shape
    n_cols = w.shape[-1] - col_off if n_cols is None else n_cols
    bm = _row_block(t, w)
    bn = _pick(math.gcd(n_cols, col_off) if col_off else n_cols, _col_cands(w))
    ob = col_off // bn
    return pl.pallas_call(
        _mm_plain_kernel,
        grid=(t // bm, n_cols // bn),
        in_specs=[pl.BlockSpec((bm, k), lambda i, j: (i, 0)), _wspec(w, layer, bn, ob)],
        out_specs=pl.BlockSpec((bm, bn), lambda i, j: (i, j)),
        out_shape=jax.ShapeDtypeStruct((t, n_cols), out_dtype),
        compiler_params=_cparams(("parallel", "arbitrary"),
                                 _vmem([((bm, k), BF16), ((k, bn), w.dtype), ((bm, bn), F32)])),
        name="matmul",
    )(x, w)


def _qkv_kernel(x_ref, w_ref, cos_ref, s1_ref, s2_ref, o_ref, *, nb_region, bn, q_scale):
    acc = jnp.dot(x_ref[...], w_ref[...].astype(BF16), preferred_element_type=F32)
    region = pl.program_id(1) // nb_region

    @pl.when(region == 1)
    def _():
        o_ref[...] = acc.astype(BF16)

    @pl.when(region != 1)
    def _():
        reps = bn // LANES
        cos = jnp.tile(cos_ref[...], (1, reps))
        s1 = jnp.tile(s1_ref[...], (1, reps))
        s2 = jnp.tile(s2_ref[...], (1, reps))
        out = acc * cos + pltpu.roll(acc, 16, axis=1) * s1 + pltpu.roll(acc, bn - 16, axis=1) * s2
        scale = jnp.where(region == 2, q_scale, 1.0).astype(F32)
        o_ref[...] = (out * scale).astype(BF16)


def _qkv_proj(a, w_in, layer, cos_t, s1_t, s2_t):
    t, k = a.shape
    bm = _row_block(t, w_in)
    bn = _pick(A_WIDTH, _col_cands(w_in))
    n_cols = 3 * A_WIDTH
    return pl.pallas_call(
        functools.partial(_qkv_kernel, nb_region=A_WIDTH // bn, bn=bn, q_scale=A_HEAD_DIM ** -0.5 * math.log2(math.e)),
        grid=(t // bm, n_cols // bn),
        in_specs=[pl.BlockSpec((bm, k), lambda i, j: (i, 0)),
                  _wspec(w_in, layer, bn, 0),
                  pl.BlockSpec((bm, LANES), lambda i, j: (i, 0)),
                  pl.BlockSpec((bm, LANES), lambda i, j: (i, 0)),
                  pl.BlockSpec((bm, LANES), lambda i, j: (i, 0))],
        out_specs=pl.BlockSpec((bm, bn), lambda i, j: (i, j)),
        out_shape=jax.ShapeDtypeStruct((t, n_cols), BF16),
        compiler_params=_cparams(("parallel", "arbitrary"),
                                 _vmem([((bm, k), BF16), ((k, bn), w_in.dtype), ((bm, bn), F32),
                                        ((bm, bn), F32)])),
        name="qkv_rope_proj",
    )(a, w_in, cos_t, s1_t, s2_t)


def _glu_kernel(x_ref, wa_ref, wg_ref, o_ref):
    x = x_ref[...]
    a = jnp.dot(x, wa_ref[...].astype(BF16), preferred_element_type=F32)
    g = jnp.dot(x, wg_ref[...].astype(BF16), preferred_element_type=F32)
    o_ref[...] = (a * _sigmoid(g)).astype(o_ref.dtype)


def _swiglu_up_kernel(x_ref, w1_ref, w3_ref, o_ref):
    x = x_ref[...]
    a = jnp.dot(x, w1_ref[...].astype(BF16), preferred_element_type=F32)
    b = jnp.dot(x, w3_ref[...].astype(BF16), preferred_element_type=F32)
    o_ref[...] = (a * _sigmoid(a) * b).astype(o_ref.dtype)


def _mm_pair(kern, x, wa, wg, out_dtype, *, off_a=0, off_g=0, n_cols=None, layer=0, name="matmul_pair"):
    t, k = x.shape
    n_cols = wa.shape[-1] if n_cols is None else n_cols
    bm = _row_block(t, wa)
    g = n_cols
    for o in (off_a, off_g):
        g = math.gcd(g, o) if o else g
    bn = _pick(g, _col_cands(wa))
    oa, og = off_a // bn, off_g // bn
    return pl.pallas_call(
        kern,
        grid=(t // bm, n_cols // bn),
        in_specs=[pl.BlockSpec((bm, k), lambda i, j: (i, 0)),
                  _wspec(wa, layer, bn, oa),
                  _wspec(wg, layer, bn, og)],
        out_specs=pl.BlockSpec((bm, bn), lambda i, j: (i, j)),
        out_shape=jax.ShapeDtypeStruct((t, n_cols), out_dtype),
        compiler_params=_cparams(("parallel", "arbitrary"),
                                 _vmem([((bm, k), BF16), ((k, bn), wa.dtype), ((k, bn), wg.dtype), ((bm, bn), F32),
                                        ((bm, bn), F32)])),
        name=name,
    )(x, wa, wg)


def _resid1_kernel(x_ref, w_ref, h_ref, gate_ref, o_ref, *, bm, n_lat):
    acc = jnp.dot(x_ref[...], w_ref[...], preferred_element_type=F32)
    is_lat = _row_is_lat(pl.program_id(0), bm, n_lat)
    gate = jnp.where(is_lat, gate_ref[0:1, :], gate_ref[1:2, :])
    o_ref[...] = h_ref[...] + gate * acc


def _resid2_kernel(x1_ref, w1_ref, x2_ref, w2_ref, h_ref, gate_ref, o_ref, *, bm, n_lat):
    acc = jnp.dot(x1_ref[...], w1_ref[...], preferred_element_type=F32)
    acc = acc + jnp.dot(x2_ref[...], w2_ref[...], preferred_element_type=F32)
    is_lat = _row_is_lat(pl.program_id(0), bm, n_lat)
    gate = jnp.where(is_lat, gate_ref[0:1, :], gate_ref[1:2, :])
    o_ref[...] = h_ref[...] + gate * acc


def _mm_resid(xs, ws, h, mods, k_gate, n_lat, *, bn_cands=(512, 256, 128), name="matmul_resid"):
    t, d = h.shape
    bm = _pick(t, (768, 512, 256, 128))
    bn = _pick(d, bn_cands)
    nbd = d // bn
    in_specs, args, blocks = [], [], []
    for x, w in zip(xs, ws):
        k = x.shape[1]
        in_specs += [pl.BlockSpec((bm, k), lambda i, j: (i, 0)), pl.BlockSpec((k, bn), lambda i, j: (0, j))]
        args += [x, w]
        blocks += [((bm, k), BF16), ((k, bn), BF16)]
    in_specs += [pl.BlockSpec((bm, bn), lambda i, j: (i, j)),
                 pl.BlockSpec((SUBLANES, bn), lambda i, j: (0, k_gate * nbd + j))]
    args += [h, mods]
    blocks += [((bm, bn), F32), ((bm, bn), F32), ((bm, bn), F32)]
    kern = _resid1_kernel if len(xs) == 1 else _resid2_kernel
    return pl.pallas_call(
        functools.partial(kern, bm=bm, n_lat=n_lat),
        grid=(t // bm, d // bn),
        in_specs=in_specs,
        out_specs=pl.BlockSpec((bm, bn), lambda i, j: (i, j)),
        out_shape=jax.ShapeDtypeStruct((t, d), F32),
        compiler_params=_cparams(("parallel", "arbitrary"), _vmem(blocks)),
        name=name,
    )(*args)


def _attn_kernel(lam_ref, subw_ref, q_ref, k_ref, v_ref, o_ref, s_scr, p_scr, a_scr, acc_scr, m_scr,
                 *, bq, bk, n_lat, n_ctx, lam_init, latent):
    q = q_ref[...]
    lane = lax.broadcasted_iota(jnp.int32, (bq, LANES), 1)
    zero = jnp.zeros_like(q)
    qm = jnp.concatenate([jnp.where(lane < A_HEAD_DIM, q, zero),
                          jnp.where(lane >= A_HEAD_DIM, q, zero)], axis=0)

    n_chunks = n_lat // bk
    rg = 64

    def scores(slot, kc):
        w = kc.shape[0]
        s_scr[slot, :, 0:w] = lax.dot_general(qm, kc, (((1,), (1,)), ((), ())), preferred_element_type=F32)

    def softmax(slot, w):
        for r0 in range(0, 2 * bq, rg):
            s = s_scr[slot, r0:r0 + rg, 0:w]
            m_old = m_scr[r0:r0 + rg, :]
            m_new = jnp.maximum(m_old, jnp.max(s, axis=1, keepdims=True))
            m_scr[r0:r0 + rg, :] = m_new
            a_scr[slot, r0:r0 + rg, :] = jnp.exp2(m_old - m_new)
            p_scr[slot, r0:r0 + rg, 0:w] = jnp.exp2(s - m_new).astype(BF16)

    def accumulate(slot, w, vc):
        vext = jnp.concatenate([vc, jnp.ones_like(vc)], axis=1)
        acc_scr[...] = (a_scr[slot] * acc_scr[...]
                        + jnp.dot(p_scr[slot, :, 0:w], vext, preferred_element_type=F32))

    def kchunk(c):
        return k_ref[pl.ds(pl.multiple_of(c * bk, bk), bk), :]

    def vchunk(c):
        return v_ref[pl.ds(pl.multiple_of(c * bk, bk), bk), :]

    m_scr[...] = jnp.full(m_scr.shape, NEG_BIG, F32)
    acc_scr[...] = jnp.zeros(acc_scr.shape, F32)
    if not latent:
        scores(0, k_ref[...])
        softmax(0, n_ctx)
        accumulate(0, n_ctx, v_ref[...])
    else:
        scores(1, k_ref[n_lat:n_lat + n_ctx, :])
        scores(0, k_ref[0:bk, :])
        softmax(1, n_ctx)
        accumulate(1, n_ctx, v_ref[n_lat:n_lat + n_ctx, :])
        softmax(0, bk)
        scores(1, k_ref[bk:2 * bk, :])

        def body(j, carry):
            accumulate(0, bk, vchunk(2 * j))
            softmax(1, bk)
            scores(0, kchunk(2 * j + 2))
            accumulate(1, bk, vchunk(2 * j + 1))
            softmax(0, bk)
            scores(1, kchunk(2 * j + 3))
            return carry

        lax.fori_loop(0, n_chunks // 2 - 1, body, 0)
        accumulate(0, bk, v_ref[(n_chunks - 2) * bk:(n_chunks - 1) * bk, :])
        softmax(1, bk)
        accumulate(1, bk, v_ref[(n_chunks - 1) * bk:n_chunks * bk, :])

    acc = acc_scr[...]
    lam = lam_ref[...]
    lam_full = (jnp.exp(jnp.sum(lam[0:1] * lam[1:2], axis=1, keepdims=True))
                - jnp.exp(jnp.sum(lam[2:3] * lam[3:4], axis=1, keepdims=True)) + lam_init)
    o = acc[:bq, :LANES] / acc[:bq, LANES:] - lam_full * (acc[bq:, :LANES] / acc[bq:, LANES:])
    y = o * lax.rsqrt(jnp.mean(o * o, axis=-1, keepdims=True) + 1e-5) * subw_ref[...]
    o_ref[...] = (y * (1.0 - lam_init)).astype(BF16)


def _attention(kvq, lam, subw, n_lat, n_ctx, lam_init):
    t = kvq.shape[0]
    bk = _pick(n_lat // 2, (512, 256, 128))
    nh = A_HEADS
    nb_ctx = n_lat // n_ctx

    def call(latent, bq, n_rows, q_map, kv_rows, k_map, v_map):
        return pl.pallas_call(
            functools.partial(_attn_kernel, bq=bq, bk=bk, n_lat=n_lat, n_ctx=n_ctx, lam_init=lam_init,
                              latent=latent),
            grid=(nh, n_rows // bq),
            in_specs=[pl.BlockSpec((4, A_HEAD_DIM), lambda h, i: (0, 0)),
                      pl.BlockSpec((1, LANES), lambda h, i: (0, 0)),
                      pl.BlockSpec((bq, LANES), q_map),
                      pl.BlockSpec((kv_rows, LANES), k_map),
                      pl.BlockSpec((kv_rows, LANES), v_map)],
            out_specs=pl.BlockSpec((bq, LANES), lambda h, i: (i, h)),
            out_shape=jax.ShapeDtypeStruct((n_rows, A_WIDTH), BF16),
            scratch_shapes=[pltpu.VMEM((2, 2 * bq, bk), F32), pltpu.VMEM((2, 2 * bq, bk), BF16),
                            pltpu.VMEM((2, 2 * bq, 1), F32), pltpu.VMEM((2 * bq, 2 * LANES), F32),
                            pltpu.VMEM((2 * bq, 1), F32)],
            compiler_params=_cparams(("parallel", "arbitrary"),
                                     _vmem([((kv_rows, LANES), BF16), ((kv_rows, LANES), BF16)],
                                           scratch=24 << 20)),
            name="diff_attention" if latent else "diff_attention_ctx",
        )(lam, subw.reshape(1, LANES), kvq, kvq, kvq)

    bq = _pick(n_lat, (512, 256, 128))
    att_lat = call(True, bq, n_lat, lambda h, i: (i, 2 * nh + h), t, lambda h, i: (0, h), lambda h, i: (0, nh + h))
    att_ctx = call(False, n_ctx, n_ctx, lambda h, i: (nb_ctx, 2 * nh + h), n_ctx,
                   lambda h, i: (nb_ctx, h), lambda h, i: (nb_ctx, nh + h))
    return jnp.concatenate([att_lat, att_ctx], axis=0)


def _dwconv_shifts(taps):
    base = HALO - (taps - 1) // 2
    return base, sorted({(base + k) % SUBLANES for k in range(taps)} - {0})


def _dwconv_kernel(prev_ref, cur_ref, next_ref, w_ref, b_ref, lnw_ref, lnb_ref, o_ref, xx_ref, sh_ref, y_ref,
                   *, bm, cb, taps, nb_lat, nb_all, layernorm):
    i = pl.program_id(0)
    has_prev = jnp.logical_and(i != 0, i != nb_lat)
    has_next = jnp.logical_and(i != nb_lat - 1, i != nb_all - 1)
    xx_ref[0:HALO, :] = jnp.where(has_prev, prev_ref[...], 0.0)
    xx_ref[HALO:HALO + bm, :] = cur_ref[...]
    xx_ref[HALO + bm:2 * HALO + bm, :] = jnp.where(has_next, next_ref[...], 0.0)
    base, shifts = _dwconv_shifts(taps)
    nr = bm + 2 * HALO - SUBLANES
    rs = min(bm, 64)
    cs = min(cb, 512)
    for c0 in range(0, cb, cs):
        for n, b in enumerate(shifts):
            sh_ref[n, 0:nr, c0:c0 + cs] = xx_ref[b:b + nr, c0:c0 + cs]
        wts = [w_ref[k:k + 1, c0:c0 + cs] for k in range(taps)]
        bias = b_ref[:, c0:c0 + cs]
        for r0 in range(0, bm, rs):
            acc = jnp.broadcast_to(bias, (rs, cs))
            for k in range(taps):
                b = (base + k) % SUBLANES
                lo = r0 + base + k - b
                if b == 0:
                    win = xx_ref[lo:lo + rs, c0:c0 + cs]
                else:
                    win = sh_ref[shifts.index(b), lo:lo + rs, c0:c0 + cs]
                acc = acc + win * wts[k]
            y_ref[r0:r0 + rs, c0:c0 + cs] = acc
    y = y_ref[...]
    if layernorm:
        mu = jnp.mean(y, axis=-1, keepdims=True)
        var = jnp.mean(jnp.square(y - mu), axis=-1, keepdims=True)
        y = (y - mu) * lax.rsqrt(var + LN_EPS) * lnw_ref[...] + lnb_ref[...]
    o_ref[...] = (y * _sigmoid(y)).astype(o_ref.dtype)


def _dwconv_silu(x, w, b, n_lat, out_dtype, *, col_off=0, n_cols=None, ln=None):
    t = x.shape[0]
    taps = w.shape[0]
    n_cols = x.shape[1] - col_off if n_cols is None else n_cols
    bm = _pick(math.gcd(n_lat, t - n_lat), (256, 128))
    cb = n_cols if ln is not None else _pick(math.gcd(n_cols, col_off) if col_off else n_cols, (1024, 512, 256, 128))
    ob = col_off // cb
    hb = bm // HALO
    n_halo_blocks = t // HALO
    lnw, lnb = ln if ln is not None else (jnp.ones((n_cols,), F32), jnp.zeros((n_cols,), F32))
    n_shifts = len(_dwconv_shifts(taps)[1])
    return pl.pallas_call(
        functools.partial(_dwconv_kernel, bm=bm, cb=cb, taps=taps, nb_lat=n_lat // bm, nb_all=t // bm,
                          layernorm=ln is not None),
        grid=(t // bm, n_cols // cb),
        in_specs=[pl.BlockSpec((HALO, cb), lambda i, j: (jnp.maximum(i * hb - 1, 0), j + ob)),
                  pl.BlockSpec((bm, cb), lambda i, j: (i, j + ob)),
                  pl.BlockSpec((HALO, cb), lambda i, j: (jnp.minimum((i + 1) * hb, n_halo_blocks - 1), j + ob)),
                  pl.BlockSpec((taps, cb), lambda i, j: (0, j)),
                  pl.BlockSpec((1, cb), lambda i, j: (0, j)),
                  pl.BlockSpec((1, cb), lambda i, j: (0, j)),
                  pl.BlockSpec((1, cb), lambda i, j: (0, j))],
        out_specs=pl.BlockSpec((bm, cb), lambda i, j: (i, j)),
        out_shape=jax.ShapeDtypeStruct((t, n_cols), out_dtype),
        scratch_shapes=[pltpu.VMEM((bm + 2 * HALO, cb), F32),
                        pltpu.VMEM((n_shifts, bm + 2 * HALO - SUBLANES, cb), F32),
                        pltpu.VMEM((bm, cb), F32)],
        compiler_params=_cparams(("parallel", "parallel"),
                                 _vmem([((bm, cb), F32), ((bm, cb), F32)],
                                       scratch=(n_shifts + 2) * (bm + 2 * HALO) * cb * 4)),
        name="dwconv_silu",
    )(x, x, x, w, b.reshape(1, n_cols), lnw.reshape(1, n_cols), lnb.reshape(1, n_cols))


def _s5_build_kernel(u_ref, lam_ref, step_ref, bt_ref, c_ref, yz_ref, et_ref, adv_ref):
    L, H, P = S5_CHUNK, S5_GROUP, S5_STATE
    lane = lax.broadcasted_iota(jnp.int32, (1, 2 * P), 1)
    first = lane < P
    sgn = jnp.where(first, -1.0, 1.0).astype(F32)
    conj_sgn = -sgn

    def swap(x):
        return pltpu.roll(x, P, axis=1)

    def dupr(x):
        return jnp.where(first, x, swap(x))

    def dupi(x):
        return jnp.where(first, swap(x), x)

    def cmul(a, b):
        return dupr(a) * b + (sgn * dupi(a)) * swap(b)

    one = jnp.where(first, 1.0, 0.0).astype(F32)
    mats, ets, advs = [], [], []
    for d in range(2):
        lam = lam_ref[0, d:d + 1, :]
        z = lam * jnp.exp(step_ref[0, d:d + 1, :])
        ang = dupi(z)
        lbar = jnp.exp(dupr(z)) * jnp.where(first, jnp.cos(ang), jnp.sin(ang))
        den = dupr(lam) * dupr(lam) + dupi(lam) * dupi(lam)
        qcoef = cmul(lbar - one, lam * conj_sgn) / den
        bbar = cmul(qcoef, bt_ref[0])
        cmat = c_ref[0, d * H:(d + 1) * H, :]
        pw = [one]
        for _ in range(L):
            pw.append(cmul(pw[-1], lbar))
        cpow = [cmul(pw[t], cmat) * conj_sgn for t in range(L + 1)]
        if d == 0:
            kern_rows = jnp.concatenate([cpow[t] for t in range(L)], axis=0)
            inj = jnp.concatenate([cmul(pw[L - 1 - j], bbar) for j in range(L)], axis=0)
            read = jnp.concatenate([cpow[t + 1] for t in range(L)], axis=0)
        else:
            kern_rows = jnp.concatenate([cpow[L - 1 - t] for t in range(L)], axis=0)
            inj = jnp.concatenate([cmul(pw[j], bbar) for j in range(L)], axis=0)
            read = jnp.concatenate([cpow[L - t] for t in range(L)], axis=0)
        base = lax.dot_general(bbar.astype(BF16), kern_rows.astype(BF16), (((1,), (1,)), ((), ())),
                               preferred_element_type=F32)
        lane_o = lax.broadcasted_iota(jnp.int32, (H, L * H), 1)
        blocks = []
        for j in range(L):
            if d == 0:
                sh = (j * H) % (L * H)
                blk = jnp.where(lane_o >= j * H, pltpu.roll(base, sh, axis=1) if sh else base, 0.0)
            else:
                sh = (L * H - (L - 1 - j) * H) % (L * H)
                blk = jnp.where(lane_o < (j + 1) * H, pltpu.roll(base, sh, axis=1) if sh else base, 0.0)
            blocks.append(blk)
        mats.append((jnp.concatenate(blocks, axis=0), inj))
        ets.append(read)
        pl_ = pw[L]
        advs.append((dupr(pl_), sgn * dupi(pl_)))
    mix = (mats[0][0] + mats[1][0]).astype(BF16)
    rhs = jnp.concatenate([mix, mats[0][1].astype(BF16), mats[1][1].astype(BF16)], axis=1)
    yz_ref[0] = jnp.dot(u_ref[0], rhs, preferred_element_type=F32)
    et_ref[0] = jnp.concatenate([ets[0], ets[1]], axis=1).astype(BF16)
    adv_ref[0] = jnp.concatenate([jnp.concatenate([advs[0][0], advs[1][0]], axis=1),
                                  jnp.concatenate([advs[0][1], advs[1][1]], axis=1)], axis=0)


def _s5_scan_kernel(z_ref, adv_ref, s_ref, *, n_chunks, nc_ctx):
    P = S5_STATE
    a1 = adv_ref[:, 0, :]
    a2 = adv_ref[:, 1, :]
    gb = a1.shape[0]
    nl = n_chunks - nc_ctx

    def swap_halves(x):
        return jnp.concatenate([pltpu.roll(x[:, :2 * P], P, axis=1), pltpu.roll(x[:, 2 * P:], P, axis=1)], axis=1)

    def body(i, carry):
        s, ssw = carry
        cf = jnp.where(i < nc_ctx, nl + i, i - nc_ctx)
        cb = n_chunks - 1 - i
        z = jnp.concatenate([z_ref[cf][:, :2 * P], z_ref[cb][:, 2 * P:]], axis=1)
        s_ref[cf, :, 0:2 * P] = s[:, :2 * P].astype(s_ref.dtype)
        s_ref[cb, :, 2 * P:4 * P] = s[:, 2 * P:].astype(s_ref.dtype)
        return a1 * s + a2 * ssw + z, a1 * ssw - a2 * s + swap_halves(z)

    zero = jnp.zeros((gb, 4 * P), F32)
    lax.fori_loop(0, n_chunks, body, (zero, zero), unroll=4)


def _s5_out_kernel(yz_ref, s_ref, et_ref, o_ref):
    n = o_ref.shape[2]
    o_ref[0] = yz_ref[0, :, 0:n] + lax.dot_general(s_ref[0], et_ref[0], (((1,), (1,)), ((), ())),
                                                  preferred_element_type=F32)


def _s5_finish_kernel(y_ref, u_ref, d_ref, w_ref, b_ref, o_ref):
    y = y_ref[...] + d_ref[...] * u_ref[...]
    g = 0.5 * y * (1.0 + jnp.tanh(math.sqrt(2.0 / math.pi) * (y + 0.044715 * (y * y * y))))
    gate = jnp.dot(g.astype(BF16), w_ref[...], preferred_element_type=F32) + b_ref[...]
    o_ref[...] = (g * _sigmoid(gate)).astype(BF16)


def _s5_branch(u, lam_re, lam_im, log_step, b_re, b_im, c_re, c_im, d_skip, glu_w, glu_b, n_lat):
    t = u.shape[0]
    G, H, P, L = S5_GROUPS, S5_GROUP, S5_STATE, S5_CHUNK
    nch = t // L
    nc_ctx = (t - n_lat) // L
    lam_p = jnp.concatenate([lam_re, lam_im], axis=-1).transpose(1, 0, 2)
    step_p = jnp.broadcast_to(log_step.T[:, :, None], (G, 2, 2 * P))
    bt_p = jnp.concatenate([b_re, b_im], axis=1).transpose(0, 2, 1)
    c_p = jnp.concatenate([c_re, c_im], axis=-1).transpose(1, 0, 2, 3).reshape(G, 2 * H, 2 * P)
    ug = u.astype(BF16).reshape(nch, L, G, H).transpose(2, 0, 1, 3).reshape(G, nch, L * H)
    LH = L * H
    yz, et, adv = pl.pallas_call(
        _s5_build_kernel,
        grid=(G,),
        in_specs=[pl.BlockSpec((1, nch, LH), lambda g: (g, 0, 0)),
                  pl.BlockSpec((1, 2, 2 * P), lambda g: (g, 0, 0)),
                  pl.BlockSpec((1, 2, 2 * P), lambda g: (g, 0, 0)),
                  pl.BlockSpec((1, H, 2 * P), lambda g: (g, 0, 0)),
                  pl.BlockSpec((1, 2 * H, 2 * P), lambda g: (g, 0, 0))],
        out_specs=[pl.BlockSpec((1, nch, LH + 4 * P), lambda g: (g, 0, 0)),
                   pl.BlockSpec((1, LH, 4 * P), lambda g: (g, 0, 0)),
                   pl.BlockSpec((1, 2, 4 * P), lambda g: (g, 0, 0))],
        out_shape=[jax.ShapeDtypeStruct((G, nch, LH + 4 * P), F32),
                   jax.ShapeDtypeStruct((G, LH, 4 * P), BF16),
                   jax.ShapeDtypeStruct((G, 2, 4 * P), F32)],
        compiler_params=_cparams(("parallel",), _vmem([((nch, LH), BF16), ((nch, LH + 4 * P), F32)])),
        name="s5_build_apply",
    )(ug, lam_p, step_p, bt_p, c_p)
    z_t = yz[:, :, LH:].transpose(1, 0, 2)
    gb = _pick(G, (16, 8))
    s_in = pl.pallas_call(
        functools.partial(_s5_scan_kernel, n_chunks=nch, nc_ctx=nc_ctx),
        grid=(G // gb,),
        in_specs=[pl.BlockSpec((nch, gb, 4 * P), lambda g: (0, g, 0)),
                  pl.BlockSpec((gb, 2, 4 * P), lambda g: (g, 0, 0))],
        out_specs=pl.BlockSpec((nch, gb, 4 * P), lambda g: (0, g, 0)),
        out_shape=jax.ShapeDtypeStruct((nch, G, 4 * P), BF16),
        compiler_params=_cparams(("parallel",), _vmem([((nch, gb, 4 * P), F32), ((nch, gb, 4 * P), BF16)])),
        name="s5_chunk_scan",
    )(z_t, adv)
    s_g = s_in.transpose(1, 0, 2)
    y = pl.pallas_call(
        _s5_out_kernel,
        grid=(G,),
        in_specs=[pl.BlockSpec((1, nch, LH + 4 * P), lambda g: (g, 0, 0)),
                  pl.BlockSpec((1, nch, 4 * P), lambda g: (g, 0, 0)),
                  pl.BlockSpec((1, LH, 4 * P), lambda g: (g, 0, 0))],
        out_specs=pl.BlockSpec((1, nch, LH), lambda g: (g, 0, 0)),
        out_shape=jax.ShapeDtypeStruct((G, nch, LH), F32),
        compiler_params=_cparams(("parallel",), _vmem([((nch, LH + 4 * P), F32), ((nch, 4 * P), BF16),
                                                       ((nch, LH), F32)])),
        name="s5_readout",
    )(yz, s_g, et)
    y_tok = y.reshape(G, nch, L, H).transpose(1, 2, 0, 3).reshape(t, G * H)
    w = G * H
    bm = _pick(t, (256, 128))
    return pl.pallas_call(
        _s5_finish_kernel,
        grid=(t // bm,),
        in_specs=[pl.BlockSpec((bm, w), lambda i: (i, 0)),
                  pl.BlockSpec((bm, w), lambda i: (i, 0)),
                  pl.BlockSpec((1, w), lambda i: (0, 0)),
                  pl.BlockSpec((w, w), lambda i: (0, 0)),
                  pl.BlockSpec((1, w), lambda i: (0, 0))],
        out_specs=pl.BlockSpec((bm, w), lambda i: (i, 0)),
        out_shape=jax.ShapeDtypeStruct((t, w), BF16),
        compiler_params=_cparams(("parallel",), _vmem([((bm, w), F32), ((bm, w), F32), ((w, w), BF16),
                                                       ((bm, w), F32)])),
        name="s5_gelu_glu",
    )(y_tok, u, d_skip.reshape(1, w), glu_w.astype(BF16), glu_b.reshape(1, w))


def _ssd_chunk(forward, xs_ref, b_ref, c_ref, dt_ref, sel_ref, bias_ref, alog_ref, y_ref, state_ref, hpg):
    L = SSD_CHUNK
    hd = M2_HEAD_DIM
    sel = sel_ref[0]
    hi, mid, lo = _split3(dt_ref[...])
    dt_raw = (jnp.dot(hi, sel, preferred_element_type=F32) + jnp.dot(mid, sel, preferred_element_type=F32)
              + jnp.dot(lo, sel, preferred_element_type=F32))
    v = dt_raw + bias_ref[0]
    dt = jnp.maximum(v, 0.0) + jnp.log1p(jnp.exp(-jnp.abs(v)))
    da = dt * (-jnp.exp(alog_ref[0]))
    ti = lax.broadcasted_iota(jnp.int32, (L, L), 0)
    si = lax.broadcasted_iota(jnp.int32, (L, L), 1)
    tri = jnp.where(si <= ti, 1.0, 0.0).astype(BF16)
    hi, mid, lo = _split3(da)
    cum = (jnp.dot(tri, hi, preferred_element_type=F32) + jnp.dot(tri, mid, preferred_element_type=F32)
           + jnp.dot(tri, lo, preferred_element_type=F32))
    total = cum[L - 1:L, :]
    if forward:
        cumq = cum
        w_out = jnp.exp(cumq)
        w_state = jnp.exp(total - cumq)
        mask = si <= ti
    else:
        cumq = cum - da
        w_out = jnp.exp(total - cumq)
        w_state = jnp.exp(cumq)
        mask = si >= ti
    xsdt = xs_ref[...] * dt
    cb_ = c_ref[...]
    bb_ = b_ref[...]
    s_old = state_ref[...]
    y_off = w_out * jnp.dot(cb_, s_old.astype(BF16), preferred_element_type=F32)
    state_ref[...] = jnp.exp(total) * s_old + lax.dot_general(
        bb_, (w_state * xsdt).astype(BF16), (((0,), (0,)), ((), ())), preferred_element_type=F32)
    cbm = lax.dot_general(cb_, bb_, (((1,), (1,)), ((), ())), preferred_element_type=F32)
    cum_t = cumq.T
    lane = lax.broadcasted_iota(jnp.int32, (L, LANES), 1)
    xb = xsdt.astype(BF16)
    zero = jnp.zeros((L, LANES), BF16)
    outs = []
    for pr in range(hpg // 2):
        gs = []
        for r in (2 * pr, 2 * pr + 1):
            col = cumq[:, hd * r:hd * r + 1]
            row = cum_t[hd * r:hd * r + 1, :]
            diff = (col - row) if forward else (row - col)
            decay = jnp.exp(jnp.where(mask, diff, NEG_BIG))
            gs.append((cbm * decay).astype(BF16))
        xp = xb[:, LANES * pr:LANES * (pr + 1)]
        rhs = jnp.concatenate([jnp.where(lane < hd, xp, zero), jnp.where(lane >= hd, xp, zero)], axis=0)
        outs.append(jnp.dot(jnp.concatenate(gs, axis=1), rhs, preferred_element_type=F32))
    y_ref[...] = jnp.concatenate(outs, axis=1) + y_off


def _ssd_kernel(*refs, hpg, gpb):
    fwd, bwd, (yf_ref, yb_ref, sf_ref, sb_ref) = refs[0:7], refs[7:14], refs[14:18]
    w = hpg * M2_HEAD_DIM

    @pl.when(pl.program_id(1) == 0)
    def _():
        sf_ref[...] = jnp.zeros_like(sf_ref)
        sb_ref[...] = jnp.zeros_like(sb_ref)

    for forward, ins, y_ref, s_ref in ((True, fwd, yf_ref, sf_ref), (False, bwd, yb_ref, sb_ref)):
        xs_ref, b_ref, c_ref, dt_ref, sel_ref, bias_ref, alog_ref = ins
        for k in range(gpb):
            cols = slice(k * w, (k + 1) * w)
            st = slice(k * M2_STATE, (k + 1) * M2_STATE)
            _ssd_chunk(forward, xs_ref.at[:, cols], b_ref.at[:, st], c_ref.at[:, st], dt_ref,
                       sel_ref.at[k:k + 1], bias_ref.at[k:k + 1], alog_ref.at[k:k + 1],
                       y_ref.at[:, cols], s_ref.at[k], hpg)


def _ssd_finish_kernel(yf_ref, yb_ref, xs_ref, z_ref, d_ref, nw_ref, o_ref, *, n_groups):
    y = yf_ref[...] + yb_ref[...] + d_ref[...] * xs_ref[...]
    z = z_ref[...]
    y = y * (z * _sigmoid(z))
    gw = y.shape[1] // n_groups
    parts = []
    for g in range(n_groups):
        seg = y[:, g * gw:(g + 1) * gw]
        parts.append(seg * lax.rsqrt(jnp.mean(seg * seg, axis=-1, keepdims=True) + RMS_EPS))
    o_ref[...] = (jnp.concatenate(parts, axis=1) * nw_ref[...]).astype(BF16)


def _ssd_branch(xs, bm_, cm_, dt_raw, z, a_log, dt_bias, d_skip, norm_w, n_lat):
    t, inner = xs.shape
    L = SSD_CHUNK
    ng = M2_GROUPS
    hpg = M2_HEADS // ng
    w = hpg * M2_HEAD_DIM
    nch = t // L
    nc_ctx = (t - n_lat) // L
    nl = nch - nc_ctx
    src = (jnp.arange(2)[:, None, None] * M2_HEADS + jnp.arange(ng)[None, :, None] * hpg
           + jnp.arange(w)[None, None, :] // M2_HEAD_DIM)
    sel = (jnp.arange(LANES)[None, None, :, None] == src[:, :, None, :]).astype(BF16)
    expand = lambda p: jnp.repeat(p, M2_HEAD_DIM, axis=-1).reshape(2, ng, 1, w)
    bias_e = expand(dt_bias)
    alog_e = expand(a_log)

    order = (lambda c: jnp.where(c < nc_ctx, nl + c, c - nc_ctx), lambda c: nch - 1 - c)
    gpb = 2 if ng % 2 == 0 else 1
    in_specs, args = [], []
    for d in range(2):
        ch = order[d]
        in_specs += [pl.BlockSpec((L, gpb * w), lambda g, c, ch=ch: (ch(c), g)),
                     pl.BlockSpec((L, gpb * M2_STATE), lambda g, c, ch=ch: (ch(c), g)),
                     pl.BlockSpec((L, gpb * M2_STATE), lambda g, c, ch=ch: (ch(c), g)),
                     pl.BlockSpec((L, LANES), lambda g, c, ch=ch: (ch(c), 0)),
                     pl.BlockSpec((gpb, LANES, w), lambda g, c: (g, 0, 0)),
                     pl.BlockSpec((gpb, 1, w), lambda g, c: (g, 0, 0)),
                     pl.BlockSpec((gpb, 1, w), lambda g, c: (g, 0, 0))]
        args += [xs, bm_, cm_, dt_raw, sel[d], bias_e[d], alog_e[d]]
    yf, yb = pl.pallas_call(
        functools.partial(_ssd_kernel, hpg=hpg, gpb=gpb),
        grid=(ng // gpb, nch),
        in_specs=in_specs,
        out_specs=[pl.BlockSpec((L, gpb * w), lambda g, c: (order[0](c), g)),
                   pl.BlockSpec((L, gpb * w), lambda g, c: (order[1](c), g))],
        out_shape=[jax.ShapeDtypeStruct((t, inner), F32), jax.ShapeDtypeStruct((t, inner), F32)],
        scratch_shapes=[pltpu.VMEM((gpb, M2_STATE, w), F32), pltpu.VMEM((gpb, M2_STATE, w), F32)],
        compiler_params=_cparams(("parallel", "arbitrary"), 32 << 20),
        name="ssd_chunk_scan",
    )(*args)
    bm = _pick(t, (256, 128))
    d_e = jnp.repeat(d_skip, M2_HEAD_DIM).reshape(1, inner)
    return pl.pallas_call(
        functools.partial(_ssd_finish_kernel, n_groups=ng),
        grid=(t // bm,),
        in_specs=[pl.BlockSpec((bm, inner), lambda i: (i, 0)),
                  pl.BlockSpec((bm, inner), lambda i: (i, 0)),
                  pl.BlockSpec((bm, inner), lambda i: (i, 0)),
                  pl.BlockSpec((bm, inner), lambda i: (i, 0)),
                  pl.BlockSpec((1, inner), lambda i: (0, 0)),
                  pl.BlockSpec((1, inner), lambda i: (0, 0))],
        out_specs=pl.BlockSpec((bm, inner), lambda i: (i, 0)),
        out_shape=jax.ShapeDtypeStruct((t, inner), BF16),
        compiler_params=_cparams(("parallel",), _vmem([((bm, inner), F32)] * 5)),
        name="ssd_gate_norm",
    )(yf, yb, xs, z, d_e, norm_w.reshape(1, inner))


def _rope_tables(n_lat, n_ctx):
    n_rows = n_lat // GRID_W
    rows = jnp.repeat(jnp.arange(n_rows), GRID_W).astype(F32)
    cols = jnp.tile(jnp.arange(GRID_W), n_rows).astype(F32)
    inv = jnp.power(ROPE_BASE, -jnp.arange(0, ROPE_AXIS_DIM, 2, dtype=F32) / ROPE_AXIS_DIM)
    ang_r = rows[:, None] * inv
    ang_c = cols[:, None] * inv
    ang = jnp.concatenate([ang_r, ang_r, ang_c, ang_c], axis=-1)
    cos, sin = jnp.cos(ang), jnp.sin(ang)
    half = ROPE_AXIS_DIM // 2
    first = (jnp.arange(A_HEAD_DIM) % ROPE_AXIS_DIM) < half
    s1 = jnp.where(first, 0.0, sin)
    s2 = jnp.where(first, -sin, 0.0)
    pad = lambda tbl, fill: jnp.concatenate([tbl, jnp.full((n_ctx, A_HEAD_DIM), fill, F32)], axis=0)
    two = lambda tbl: jnp.concatenate([tbl, tbl], axis=-1)
    return two(pad(cos, 1.0)), two(pad(s1, 0.0)), two(pad(s2, 0.0))


def _swiglu_ffn(h, mods, nw, w1_all, w3_all, layer, w2, n_lat):
    a = _normmod(h, nw, mods, 3, 4, n_lat)
    hid = _mm_pair(_swiglu_up_kernel, a, w1_all, w3_all, BF16, layer=layer, name="ffn_gate_up")
    return _mm_resid([hid], [w2], h, mods, 5, n_lat, bn_cands=(256, 128), name="ffn_down_resid")


def _even_layer(h, a, mods, w_in_all, j, w_out, lam, subw, conv_w, conv_b, ln_w, ln_b, rope, layer_idx, n_lat):
    n_ctx = h.shape[0] - n_lat
    kvq = _qkv_proj(a, w_in_all, j, *rope)
    glu = _mm_pair(_glu_kernel, a, w_in_all, w_in_all, F32, off_a=EV_B0, off_g=EV_B0 + B_WIDTH, n_cols=B_WIDTH,
                   layer=j, name="conv_glu_proj")
    lam_init = 0.8 - 0.6 * math.exp(-0.3 * layer_idx)
    att = _attention(kvq, lam, subw, n_lat, n_ctx, lam_init)
    cv = _dwconv_silu(glu, conv_w, conv_b, n_lat, BF16, ln=(ln_w, ln_b))
    w_out = w_out.astype(BF16)
    return _mm_resid([att, cv], [w_out[:A_WIDTH], w_out[A_WIDTH:]], h, mods, 2, n_lat, name="even_out_resid")


def _odd_layer(h, a, mods, w_in_all, j, w_out, s5p, m2p, n_lat):
    w_in = w_in_all[j]
    u = _mm_plain(a, w_in_all, F32, col_off=0, n_cols=S5_WIDTH, layer=j)
    xbc = _mm_plain(a, w_in_all, F32, col_off=S5_WIDTH, n_cols=M2_CONV_DIM, layer=j)
    n_dt = 2 * M2_HEADS
    w_dt = jnp.pad(w_in[:, OD_DT0:OD_Z0], ((0, 0), (0, LANES - n_dt))).astype(BF16)
    dt_raw = _mm_plain(a, w_dt, F32)
    z = _mm_plain(a, w_in[:, OD_Z0:].astype(BF16), F32)
    s5_out = _s5_branch(u, *s5p, n_lat)
    conv_w, conv_b, a_log, dt_bias, m2_d, m2_norm_w = m2p
    gn = M2_GROUPS * M2_STATE
    xs = _dwconv_silu(xbc, conv_w[:, :M2_INNER], conv_b[:M2_INNER], n_lat, F32, col_off=0, n_cols=M2_INNER)
    bmat = _dwconv_silu(xbc, conv_w[:, M2_INNER:M2_INNER + gn], conv_b[M2_INNER:M2_INNER + gn], n_lat, BF16,
                        col_off=M2_INNER, n_cols=gn)
    cmat = _dwconv_silu(xbc, conv_w[:, M2_INNER + gn:], conv_b[M2_INNER + gn:], n_lat, BF16,
                        col_off=M2_INNER + gn, n_cols=gn)
    ssd_out = _ssd_branch(xs, bmat, cmat, dt_raw, z, a_log, dt_bias, m2_d, m2_norm_w, n_lat)
    w_out = w_out.astype(BF16)
    return _mm_resid([s5_out, ssd_out], [w_out[:S5_WIDTH], w_out[S5_WIDTH:]], h, mods, 2, n_lat,
                     name="odd_out_resid")


def kernel(x, c, ctx, c_ctx, ada_w, ada_b, norm_w, ffn_w1, ffn_w3, ffn_w2, ev_w_in, ev_w_out, ev_lambda, ev_subln_w, ev_conv_w, ev_conv_b, ev_ln_w, ev_ln_b, od_w_in, od_w_out, s5_lam_re, s5_lam_im, s5_log_step, s5_b_re, s5_b_im, s5_c_re, s5_c_im, s5_d, s5_glu_w, s5_glu_b, m2_conv_w, m2_conv_b, m2_a_log, m2_dt_bias, m2_d, m2_norm_w, final_norm_w):
    assert x.shape[0] == 1 and c.shape[0] == 1 and ctx.shape[0] == 1
    n_lat, n_ctx = x.shape[1], ctx.shape[1]
    depth = ada_w.shape[0]
    mods_all = _ada_mods(c, c_ctx, ada_w, ada_b)
    rope = _rope_tables(n_lat, n_ctx)
    h = None
    for i in range(depth):
        mods = mods_all[i]
        j = i // 2
        if i == 0:
            a, h = _normmod_first(x[0], ctx[0], norm_w[i, 0], mods, 0, 1)
        else:
            a = _normmod(h, norm_w[i, 0], mods, 0, 1, n_lat)
        if i % 2 == 0:
            h = _even_layer(h, a, mods, ev_w_in, j, ev_w_out[j], ev_lambda[j], ev_subln_w[j],
                            ev_conv_w[j], ev_conv_b[j], ev_ln_w[j], ev_ln_b[j], rope, i, n_lat)
        else:
            s5p = (s5_lam_re[j], s5_lam_im[j], s5_log_step[j], s5_b_re[j], s5_b_im[j], s5_c_re[j], s5_c_im[j],
                   s5_d[j], s5_glu_w[j], s5_glu_b[j])
            m2p = (m2_conv_w[j], m2_conv_b[j], m2_a_log[j], m2_dt_bias[j], m2_d[j], m2_norm_w[j])
            h = _odd_layer(h, a, mods, od_w_in, j, od_w_out[j], s5p, m2p, n_lat)
        h = _swiglu_ffn(h, mods, norm_w[i, 1], ffn_w1, ffn_w3, i, ffn_w2[i].astype(BF16), n_lat)
    return _final_norm(h, final_norm_w, n_lat)[None]
```

```python
import functools
import math

import jax
import jax.numpy as jnp
from jax import lax
from jax.experimental import pallas as pl
from jax.experimental.pallas import tpu as pltpu

F32 = jnp.float32
BF16 = jnp.bfloat16

D_MODEL = 4096
SEQ = 8192
DEPTH = 2
CTX_LEN = 256
GRID_W = 64
RMS_EPS = 1e-6
LN_EPS = 1e-5
A_WIDTH = D_MODEL // 2
A_HEAD_DIM = 64
A_HEADS = A_WIDTH // (2 * A_HEAD_DIM)
B_WIDTH = D_MODEL - A_WIDTH
CONV_WIDTH = 31
ROPE_BASE = 10000.0
ROPE_AXIS_DIM = A_HEAD_DIM // 2
EV_V0 = A_WIDTH
EV_Q0 = 2 * A_WIDTH
EV_B0 = 3 * A_WIDTH
S5_WIDTH = D_MODEL // 4
S5_GROUP = 16
S5_GROUPS = S5_WIDTH // S5_GROUP
S5_STATE = 64
M2_INNER = D_MODEL - S5_WIDTH
M2_HEAD_DIM = 64
M2_HEADS = M2_INNER // M2_HEAD_DIM
M2_GROUPS = 8
M2_STATE = 128
M2_CONV = 5
M2_CONV_DIM = M2_INNER + 2 * M2_GROUPS * M2_STATE
OD_DT0 = S5_WIDTH + M2_CONV_DIM
OD_Z0 = OD_DT0 + 2 * M2_HEADS

V7X_VMEM_BYTES = 64 * 1024 * 1024
LANES = 128
SUBLANES = 8
HALO = 16
S5_CHUNK = 16
SSD_CHUNK = 128
NEG_BIG = -1e30


def _cparams(sem, vmem_bytes):
    return pltpu.CompilerParams(dimension_semantics=sem,
                                vmem_limit_bytes=int(min(vmem_bytes, V7X_VMEM_BYTES - (4 << 20))))


def _pick(n, cands):
    for c in cands:
        if n % c == 0:
            return c
    raise ValueError(f"no block size in {cands} divides {n}")


def _nbytes(shape, dtype):
    return math.prod(shape) * jnp.dtype(dtype).itemsize


def _vmem(blocks, scratch=0):
    return 2 * sum(_nbytes(s, d) for s, d in blocks) + scratch + (12 << 20)


def _sigmoid(x):
    return jax.nn.sigmoid(x)


def _split3(x):
    hi = x.astype(BF16)
    r1 = x - hi.astype(F32)
    mid = r1.astype(BF16)
    lo = (r1 - mid.astype(F32)).astype(BF16)
    return hi, mid, lo


def _row_is_lat(i, bm, n_lat):
    row = i * bm + lax.broadcasted_iota(jnp.int32, (bm, 1), 0)
    return row < n_lat


def _ada_kernel(x_ref, w_ref, b_ref, o_ref):
    x = x_ref[...]
    xs = (x * _sigmoid(x)).astype(BF16)
    o_ref[0] = jnp.dot(xs, w_ref[0].astype(BF16), preferred_element_type=F32) + b_ref[0]


def _ada_mods(c, c_ctx, ada_w, ada_b):
    depth, d, n6 = ada_w.shape
    xin = jnp.zeros((SUBLANES, d), F32).at[0].set(c[0]).at[1].set(c_ctx)
    bn = _pick(n6, (512, 256, 128))
    return pl.pallas_call(
        _ada_kernel,
        grid=(depth, n6 // bn),
        in_specs=[pl.BlockSpec((SUBLANES, d), lambda l, j: (0, 0)),
                  pl.BlockSpec((1, d, bn), lambda l, j: (l, 0, j)),
                  pl.BlockSpec((1, 1, bn), lambda l, j: (l, 0, j))],
        out_specs=pl.BlockSpec((1, SUBLANES, bn), lambda l, j: (l, 0, j)),
        out_shape=jax.ShapeDtypeStruct((depth, SUBLANES, n6), F32),
        compiler_params=_cparams(("parallel", "parallel"), _vmem([((d, bn), F32), ((d, bn), BF16)])),
        name="ada_mods",
    )(xin, ada_w, ada_b.reshape(depth, 1, n6))


def _normmod_kernel(h_ref, nw_ref, sh_ref, sc_ref, o_ref, *, bm, n_lat):
    x = h_ref[...]
    y = x * lax.rsqrt(jnp.mean(x * x, axis=-1, keepdims=True) + RMS_EPS) * nw_ref[...]
    is_lat = _row_is_lat(pl.program_id(0), bm, n_lat)
    sh = jnp.where(is_lat, sh_ref[0:1, :], sh_ref[1:2, :])
    sc = jnp.where(is_lat, sc_ref[0:1, :], sc_ref[1:2, :])
    o_ref[...] = (y * (1.0 + sc) + sh).astype(BF16)


def _normmod(h, nw, mods, k_shift, k_scale, n_lat):
    t, d = h.shape
    bm = _pick(t, (384, 256, 128))
    return pl.pallas_call(
        functools.partial(_normmod_kernel, bm=bm, n_lat=n_lat),
        grid=(t // bm,),
        in_specs=[pl.BlockSpec((bm, d), lambda i: (i, 0)),
                  pl.BlockSpec((1, d), lambda i: (0, 0)),
                  pl.BlockSpec((SUBLANES, d), lambda i: (0, k_shift)),
                  pl.BlockSpec((SUBLANES, d), lambda i: (0, k_scale))],
        out_specs=pl.BlockSpec((bm, d), lambda i: (i, 0)),
        out_shape=jax.ShapeDtypeStruct((t, d), BF16),
        compiler_params=_cparams(("parallel",), _vmem([((bm, d), F32), ((bm, d), BF16), ((bm, d), F32)])),
        name="norm_modulate",
    )(h, nw.reshape(1, d), mods, mods)


def _normmod_first_kernel(x_ref, ctx_ref, nw_ref, sh_ref, sc_ref, o_ref, h_ref, *, nb_lat):
    is_lat = pl.program_id(0) < nb_lat
    x = jnp.where(is_lat, x_ref[...], ctx_ref[...])
    h_ref[...] = x
    y = x * lax.rsqrt(jnp.mean(x * x, axis=-1, keepdims=True) + RMS_EPS) * nw_ref[...]
    sh = jnp.where(is_lat, sh_ref[0:1, :], sh_ref[1:2, :])
    sc = jnp.where(is_lat, sc_ref[0:1, :], sc_ref[1:2, :])
    o_ref[...] = (y * (1.0 + sc) + sh).astype(BF16)


def _normmod_first(x, ctx, nw, mods, k_shift, k_scale):
    (n_lat, d), n_ctx = x.shape, ctx.shape[0]
    t = n_lat + n_ctx
    bm = _pick(math.gcd(n_lat, n_ctx), (256, 128))
    nb_lat = n_lat // bm
    return pl.pallas_call(
        functools.partial(_normmod_first_kernel, nb_lat=nb_lat),
        grid=(t // bm,),
        in_specs=[pl.BlockSpec((bm, d), lambda i: (jnp.minimum(i, nb_lat - 1), 0)),
                  pl.BlockSpec((bm, d), lambda i: (jnp.maximum(i - nb_lat, 0), 0)),
                  pl.BlockSpec((1, d), lambda i: (0, 0)),
                  pl.BlockSpec((SUBLANES, d), lambda i: (0, k_shift)),
                  pl.BlockSpec((SUBLANES, d), lambda i: (0, k_scale))],
        out_specs=[pl.BlockSpec((bm, d), lambda i: (i, 0)), pl.BlockSpec((bm, d), lambda i: (i, 0))],
        out_shape=[jax.ShapeDtypeStruct((t, d), BF16), jax.ShapeDtypeStruct((t, d), F32)],
        compiler_params=_cparams(("parallel",), _vmem([((bm, d), F32)] * 4)),
        name="norm_modulate_first",
    )(x, ctx, nw.reshape(1, d), mods, mods)


def _final_norm_kernel(h_ref, nw_ref, o_ref):
    x = h_ref[...]
    o_ref[...] = x * lax.rsqrt(jnp.mean(x * x, axis=-1, keepdims=True) + RMS_EPS) * nw_ref[...]


def _final_norm(h, nw, n_lat):
    t, d = h.shape
    bm = _pick(n_lat, (256, 128))
    return pl.pallas_call(
        _final_norm_kernel,
        grid=(n_lat // bm,),
        in_specs=[pl.BlockSpec((bm, d), lambda i: (i, 0)),
                  pl.BlockSpec((1, d), lambda i: (0, 0))],
        out_specs=pl.BlockSpec((bm, d), lambda i: (i, 0)),
        out_shape=jax.ShapeDtypeStruct((n_lat, d), F32),
        compiler_params=_cparams(("parallel",), _vmem([((bm, d), F32), ((bm, d), F32), ((bm, d), F32)])),
        name="final_norm",
    )(h, nw.reshape(1, d))


def _row_block(t, w):
    return _pick(t, (1408, 768, 512, 256, 128) if w.dtype == F32 else (768, 512, 256, 128))


def _col_cands(w):
    return (256, 128) if w.dtype == F32 else (512, 256, 128)


def _wspec(w, layer, bn, ob):
    k = w.shape[-2]
    if w.ndim == 3:
        return pl.BlockSpec((None, k, bn), lambda i, j: (layer, 0, j + ob))
    return pl.BlockSpec((k, bn), lambda i, j: (0, j + ob))


def _mm_plain_kernel(x_ref, w_ref, o_ref):
    o_ref[...] = jnp.dot(x_ref[...], w_ref[...].astype(BF16), preferred_element_type=F32).astype(o_ref.dtype)


def _mm_plain(x, w, out_dtype, *, col_off=0, n_cols=None, layer=0):
    t, k = x.shape
    n_cols = w.shape[-1] - col_off if n_cols is None else n_cols
    bm = _row_block(t, w)
    bn = _pick(math.gcd(n_cols, col_off) if col_off else n_cols, _col_cands(w))
    ob = col_off // bn
    return pl.pallas_call(
        _mm_plain_kernel,
        grid=(t // bm, n_cols // bn),
        in_specs=[pl.BlockSpec((bm, k), lambda i, j: (i, 0)), _wspec(w, layer, bn, ob)],
        out_specs=pl.BlockSpec((bm, bn), lambda i, j: (i, j)),
        out_shape=jax.ShapeDtypeStruct((t, n_cols), out_dtype),
        compiler_params=_cparams(("parallel", "arbitrary"),
                                 _vmem([((bm, k), BF16), ((k, bn), w.dtype), ((bm, bn), F32)])),
        name="matmul",
    )(x, w)


def _qkv_kernel(x_ref, w_ref, cos_ref, s1_ref, s2_ref, o_ref, *, nb_region, bn, q_scale):
    acc = jnp.dot(x_ref[...], w_ref[...].astype(BF16), preferred_element_type=F32)
    region = pl.program_id(1) // nb_region

    @pl.when(region == 1)
    def _():
        o_ref[...] = acc.astype(BF16)

    @pl.when(region != 1)
    def _():
        reps = bn // LANES
        cos = jnp.tile(cos_ref[...], (1, reps))
        s1 = jnp.tile(s1_ref[...], (1, reps))
        s2 = jnp.tile(s2_ref[...], (1, reps))
        out = acc * cos + pltpu.roll(acc, 16, axis=1) * s1 + pltpu.roll(acc, bn - 16, axis=1) * s2
        scale = jnp.where(region == 2, q_scale, 1.0).astype(F32)
        o_ref[...] = (out * scale).astype(BF16)


def _qkv_proj(a, w_in, layer, cos_t, s1_t, s2_t):
    t, k = a.shape
    bm = _row_block(t, w_in)
    bn = _pick(A_WIDTH, _col_cands(w_in))
    n_cols = 3 * A_WIDTH
    return pl.pallas_call(
        functools.partial(_qkv_kernel, nb_region=A_WIDTH // bn, bn=bn, q_scale=A_HEAD_DIM ** -0.5 * math.log2(math.e)),
        grid=(t // bm, n_cols // bn),
        in_specs=[pl.BlockSpec((bm, k), lambda i, j: (i, 0)),
                  _wspec(w_in, layer, bn, 0),
                  pl.BlockSpec((bm, LANES), lambda i, j: (i, 0)),
                  pl.BlockSpec((bm, LANES), lambda i, j: (i, 0)),
                  pl.BlockSpec((bm, LANES), lambda i, j: (i, 0))],
        out_specs=pl.BlockSpec((bm, bn), lambda i, j: (i, j)),
        out_shape=jax.ShapeDtypeStruct((t, n_cols), BF16),
        compiler_params=_cparams(("parallel", "arbitrary"),
                                 _vmem([((bm, k), BF16), ((k, bn), w_in.dtype), ((bm, bn), F32),
                                        ((bm, bn), F32)])),
        name="qkv_rope_proj",
    )(a, w_in, cos_t, s1_t, s2_t)


def _glu_kernel(x_ref, wa_ref, wg_ref, o_ref):
    x = x_ref[...]
    a = jnp.dot(x, wa_ref[...].astype(BF16), preferred_element_type=F32)
    g = jnp.dot(x, wg_ref[...].astype(BF16), preferred_element_type=F32)
    o_ref[...] = (a * _sigmoid(g)).astype(o_ref.dtype)


def _swiglu_up_kernel(x_ref, w1_ref, w3_ref, o_ref):
    x = x_ref[...]
    a = jnp.dot(x, w1_ref[...].astype(BF16), preferred_element_type=F32)
    b = jnp.dot(x, w3_ref[...].astype(BF16), preferred_element_type=F32)
    o_ref[...] = (a * _sigmoid(a) * b).astype(o_ref.dtype)


def _mm_pair(kern, x, wa, wg, out_dtype, *, off_a=0, off_g=0, n_cols=None, layer=0, name="matmul_pair"):
    t, k = x.shape
    n_cols = wa.shape[-1] if n_cols is None else n_cols
    bm = _row_block(t, wa)
    g = n_cols
    for o in (off_a, off_g):
        g = math.gcd(g, o) if o else g
    bn = _pick(g, _col_cands(wa))
    oa, og = off_a // bn, off_g // bn
    return pl.pallas_call(
        kern,
        grid=(t // bm, n_cols // bn),
        in_specs=[pl.BlockSpec((bm, k), lambda i, j: (i, 0)),
                  _wspec(wa, layer, bn, oa),
                  _wspec(wg, layer, bn, og)],
        out_specs=pl.BlockSpec((bm, bn), lambda i, j: (i, j)),
        out_shape=jax.ShapeDtypeStruct((t, n_cols), out_dtype),
        compiler_params=_cparams(("parallel", "arbitrary"),
                                 _vmem([((bm, k), BF16), ((k, bn), wa.dtype), ((k, bn), wg.dtype), ((bm, bn), F32),
                                        ((bm, bn), F32)])),
        name=name,
    )(x, wa, wg)


def _resid1_kernel(x_ref, w_ref, h_ref, gate_ref, o_ref, *, bm, n_lat):
    acc = jnp.dot(x_ref[...], w_ref[...], preferred_element_type=F32)
    is_lat = _row_is_lat(pl.program_id(0), bm, n_lat)
    gate = jnp.where(is_lat, gate_ref[0:1, :], gate_ref[1:2, :])
    o_ref[...] = h_ref[...] + gate * acc


def _resid2_kernel(x1_ref, w1_ref, x2_ref, w2_ref, h_ref, gate_ref, o_ref, *, bm, n_lat):
    acc = jnp.dot(x1_ref[...], w1_ref[...], preferred_element_type=F32)
    acc = acc + jnp.dot(x2_ref[...], w2_ref[...], preferred_element_type=F32)
    is_lat = _row_is_lat(pl.program_id(0), bm, n_lat)
    gate = jnp.where(is_lat, gate_ref[0:1, :], gate_ref[1:2, :])
    o_ref[...] = h_ref[...] + gate * acc


def _mm_resid(xs, ws, h, mods, k_gate, n_lat, *, bn_cands=(512, 256, 128), layer=0, name="matmul_resid"):
    t, d = h.shape
    bm = _pick(t, (768, 512, 256, 128))
    bn = _pick(d, bn_cands)
    nbd = d // bn
    in_specs, args, blocks = [], [], []
    for x, w in zip(xs, ws):
        k = x.shape[1]
        in_specs += [pl.BlockSpec((bm, k), lambda i, j: (i, 0)), _wspec(w, layer, bn, 0)]
        args += [x, w]
        blocks += [((bm, k), BF16), ((k, bn), BF16)]
    in_specs += [pl.BlockSpec((bm, bn), lambda i, j: (i, j)),
                 pl.BlockSpec((SUBLANES, bn), lambda i, j: (0, k_gate * nbd + j))]
    args += [h, mods]
    blocks += [((bm, bn), F32), ((bm, bn), F32), ((bm, bn), F32)]
    kern = _resid1_kernel if len(xs) == 1 else _resid2_kernel
    return pl.pallas_call(
        functools.partial(kern, bm=bm, n_lat=n_lat),
        grid=(t // bm, d // bn),
        in_specs=in_specs,
        out_specs=pl.BlockSpec((bm, bn), lambda i, j: (i, j)),
        out_shape=jax.ShapeDtypeStruct((t, d), F32),
        compiler_params=_cparams(("parallel", "arbitrary"), _vmem(blocks)),
        name=name,
    )(*args)


def _attn_kernel(lam_ref, subw_ref, q_ref, k_ref, v_ref, o_ref, s_scr, p_scr, a_scr, acc_scr, m_scr,
                 *, bq, bk, n_lat, n_ctx, lam_init, latent):
    q = q_ref[...]
    lane = lax.broadcasted_iota(jnp.int32, (bq, LANES), 1)
    zero = jnp.zeros_like(q)
    qm = jnp.concatenate([jnp.where(lane < A_HEAD_DIM, q, zero),
                          jnp.where(lane >= A_HEAD_DIM, q, zero)], axis=0)

    n_chunks = n_lat // bk
    rg = 64

    def scores(slot, kc):
        w = kc.shape[0]
        s_scr[slot, :, 0:w] = lax.dot_general(qm, kc, (((1,), (1,)), ((), ())), preferred_element_type=F32)

    def softmax(slot, w):
        for r0 in range(0, 2 * bq, rg):
            s = s_scr[slot, r0:r0 + rg, 0:w]
            m_old = m_scr[r0:r0 + rg, :]
            m_new = jnp.maximum(m_old, jnp.max(s, axis=1, keepdims=True))
            m_scr[r0:r0 + rg, :] = m_new
            a_scr[slot, r0:r0 + rg, :] = jnp.exp2(m_old - m_new)
            p_scr[slot, r0:r0 + rg, 0:w] = jnp.exp2(s - m_new).astype(BF16)

    def accumulate(slot, w, vc):
        vext = jnp.concatenate([vc, jnp.ones_like(vc)], axis=1)
        acc_scr[...] = (a_scr[slot] * acc_scr[...]
                        + jnp.dot(p_scr[slot, :, 0:w], vext, preferred_element_type=F32))

    def kchunk(c):
        return k_ref[pl.ds(pl.multiple_of(c * bk, bk), bk), :]

    def vchunk(c):
        return v_ref[pl.ds(pl.multiple_of(c * bk, bk), bk), :]

    m_scr[...] = jnp.full(m_scr.shape, NEG_BIG, F32)
    acc_scr[...] = jnp.zeros(acc_scr.shape, F32)
    if not latent:
        scores(0, k_ref[...])
        softmax(0, n_ctx)
        accumulate(0, n_ctx, v_ref[...])
    else:
        scores(1, k_ref[n_lat:n_lat + n_ctx, :])
        scores(0, k_ref[0:bk, :])
        softmax(1, n_ctx)
        accumulate(1, n_ctx, v_ref[n_lat:n_lat + n_ctx, :])
        softmax(0, bk)
        scores(1, k_ref[bk:2 * bk, :])

        def body(j, carry):
            accumulate(0, bk, vchunk(2 * j))
            softmax(1, bk)
            scores(0, kchunk(2 * j + 2))
            accumulate(1, bk, vchunk(2 * j + 1))
            softmax(0, bk)
            scores(1, kchunk(2 * j + 3))
            return carry

        lax.fori_loop(0, n_chunks // 2 - 1, body, 0)
        accumulate(0, bk, v_ref[(n_chunks - 2) * bk:(n_chunks - 1) * bk, :])
        softmax(1, bk)
        accumulate(1, bk, v_ref[(n_chunks - 1) * bk:n_chunks * bk, :])

    acc = acc_scr[...]
    lam = lam_ref[...]
    lam_full = (jnp.exp(jnp.sum(lam[0:1] * lam[1:2], axis=1, keepdims=True))
                - jnp.exp(jnp.sum(lam[2:3] * lam[3:4], axis=1, keepdims=True)) + lam_init)
    o = acc[:bq, :LANES] / acc[:bq, LANES:] - lam_full * (acc[bq:, :LANES] / acc[bq:, LANES:])
    y = o * lax.rsqrt(jnp.mean(o * o, axis=-1, keepdims=True) + 1e-5) * subw_ref[...]
    o_ref[...] = (y * (1.0 - lam_init)).astype(BF16)


def _attention(kvq, lam, subw, n_lat, n_ctx, lam_init):
    t = kvq.shape[0]
    bk = _pick(n_lat // 2, (512, 256, 128))
    nh = A_HEADS
    nb_ctx = n_lat // n_ctx

    def call(latent, bq, n_rows, q_map, kv_rows, k_map, v_map):
        return pl.pallas_call(
            functools.partial(_attn_kernel, bq=bq, bk=bk, n_lat=n_lat, n_ctx=n_ctx, lam_init=lam_init,
                              latent=latent),
            grid=(nh, n_rows // bq),
            in_specs=[pl.BlockSpec((4, A_HEAD_DIM), lambda h, i: (0, 0)),
                      pl.BlockSpec((1, LANES), lambda h, i: (0, 0)),
                      pl.BlockSpec((bq, LANES), q_map),
                      pl.BlockSpec((kv_rows, LANES), k_map),
                      pl.BlockSpec((kv_rows, LANES), v_map)],
            out_specs=pl.BlockSpec((bq, LANES), lambda h, i: (i, h)),
            out_shape=jax.ShapeDtypeStruct((n_rows, A_WIDTH), BF16),
            scratch_shapes=[pltpu.VMEM((2, 2 * bq, bk), F32), pltpu.VMEM((2, 2 * bq, bk), BF16),
                            pltpu.VMEM((2, 2 * bq, 1), F32), pltpu.VMEM((2 * bq, 2 * LANES), F32),
                            pltpu.VMEM((2 * bq, 1), F32)],
            compiler_params=_cparams(("parallel", "arbitrary"),
                                     _vmem([((kv_rows, LANES), BF16), ((kv_rows, LANES), BF16)],
                                           scratch=24 << 20)),
            name="diff_attention" if latent else "diff_attention_ctx",
        )(lam, subw.reshape(1, LANES), kvq, kvq, kvq)

    bq = _pick(n_lat, (512, 256, 128))
    att_lat = call(True, bq, n_lat, lambda h, i: (i, 2 * nh + h), t, lambda h, i: (0, h), lambda h, i: (0, nh + h))
    att_ctx = call(False, n_ctx, n_ctx, lambda h, i: (nb_ctx, 2 * nh + h), n_ctx,
                   lambda h, i: (nb_ctx, h), lambda h, i: (nb_ctx, nh + h))
    return jnp.concatenate([att_lat, att_ctx], axis=0)


def _dwconv_shifts(taps):
    base = HALO - (taps - 1) // 2
    return base, sorted({(base + k) % SUBLANES for k in range(taps)} - {0})


def _dwconv_kernel(prev_ref, cur_ref, next_ref, w_ref, b_ref, lnw_ref, lnb_ref, o_ref, xx_ref, sh_ref, y_ref,
                   *, bm, cb, taps, nb_lat, nb_all, layernorm):
    i = pl.program_id(0)
    has_prev = jnp.logical_and(i != 0, i != nb_lat)
    has_next = jnp.logical_and(i != nb_lat - 1, i != nb_all - 1)
    xx_ref[0:HALO, :] = jnp.where(has_prev, prev_ref[...], 0.0)
    xx_ref[HALO:HALO + bm, :] = cur_ref[...]
    xx_ref[HALO + bm:2 * HALO + bm, :] = jnp.where(has_next, next_ref[...], 0.0)
    base, shifts = _dwconv_shifts(taps)
    nr = bm + 2 * HALO - SUBLANES
    rs = min(bm, 64)
    cs = min(cb, 512)
    for c0 in range(0, cb, cs):
        for n, b in enumerate(shifts):
            sh_ref[n, 0:nr, c0:c0 + cs] = xx_ref[b:b + nr, c0:c0 + cs]
        wts = [w_ref[k:k + 1, c0:c0 + cs] for k in range(taps)]
        bias = b_ref[:, c0:c0 + cs]
        for r0 in range(0, bm, rs):
            acc = jnp.broadcast_to(bias, (rs, cs))
            for k in range(taps):
                b = (base + k) % SUBLANES
                lo = r0 + base + k - b
                if b == 0:
                    win = xx_ref[lo:lo + rs, c0:c0 + cs]
                else:
                    win = sh_ref[shifts.index(b), lo:lo + rs, c0:c0 + cs]
                acc = acc + win * wts[k]
            y_ref[r0:r0 + rs, c0:c0 + cs] = acc
    y = y_ref[...]
    if layernorm:
        mu = jnp.mean(y, axis=-1, keepdims=True)
        var = jnp.mean(jnp.square(y - mu), axis=-1, keepdims=True)
        y = (y - mu) * lax.rsqrt(var + LN_EPS) * lnw_ref[...] + lnb_ref[...]
    o_ref[...] = (y * _sigmoid(y)).astype(o_ref.dtype)


def _dwconv_silu(x, w, b, n_lat, out_dtype, *, col_off=0, n_cols=None, ln=None):
    t = x.shape[0]
    taps = w.shape[0]
    n_cols = x.shape[1] - col_off if n_cols is None else n_cols
    bm = _pick(math.gcd(n_lat, t - n_lat), (256, 128))
    cb = n_cols if ln is not None else _pick(math.gcd(n_cols, col_off) if col_off else n_cols, (1024, 512, 256, 128))
    ob = col_off // cb
    hb = bm // HALO
    n_halo_blocks = t // HALO
    lnw, lnb = ln if ln is not None else (jnp.ones((n_cols,), F32), jnp.zeros((n_cols,), F32))
    n_shifts = len(_dwconv_shifts(taps)[1])
    return pl.pallas_call(
        functools.partial(_dwconv_kernel, bm=bm, cb=cb, taps=taps, nb_lat=n_lat // bm, nb_all=t // bm,
                          layernorm=ln is not None),
        grid=(t // bm, n_cols // cb),
        in_specs=[pl.BlockSpec((HALO, cb), lambda i, j: (jnp.maximum(i * hb - 1, 0), j + ob)),
                  pl.BlockSpec((bm, cb), lambda i, j: (i, j + ob)),
                  pl.BlockSpec((HALO, cb), lambda i, j: (jnp.minimum((i + 1) * hb, n_halo_blocks - 1), j + ob)),
                  pl.BlockSpec((taps, cb), lambda i, j: (0, j)),
                  pl.BlockSpec((1, cb), lambda i, j: (0, j)),
                  pl.BlockSpec((1, cb), lambda i, j: (0, j)),
                  pl.BlockSpec((1, cb), lambda i, j: (0, j))],
        out_specs=pl.BlockSpec((bm, cb), lambda i, j: (i, j)),
        out_shape=jax.ShapeDtypeStruct((t, n_cols), out_dtype),
        scratch_shapes=[pltpu.VMEM((bm + 2 * HALO, cb), F32),
                        pltpu.VMEM((n_shifts, bm + 2 * HALO - SUBLANES, cb), F32),
                        pltpu.VMEM((bm, cb), F32)],
        compiler_params=_cparams(("parallel", "parallel"),
                                 _vmem([((bm, cb), F32), ((bm, cb), F32)],
                                       scratch=(n_shifts + 2) * (bm + 2 * HALO) * cb * 4)),
        name="dwconv_silu",
    )(x, x, x, w, b.reshape(1, n_cols), lnw.reshape(1, n_cols), lnb.reshape(1, n_cols))


def _s5_build_kernel(u_ref, lam_ref, step_ref, bt_ref, c_ref, yz_ref, et_ref, adv_ref):
    L, H, P = S5_CHUNK, S5_GROUP, S5_STATE
    lane = lax.broadcasted_iota(jnp.int32, (1, 2 * P), 1)
    first = lane < P
    sgn = jnp.where(first, -1.0, 1.0).astype(F32)
    conj_sgn = -sgn

    def swap(x):
        return pltpu.roll(x, P, axis=1)

    def dupr(x):
        return jnp.where(first, x, swap(x))

    def dupi(x):
        return jnp.where(first, swap(x), x)

    def cmul(a, b):
        return dupr(a) * b + (sgn * dupi(a)) * swap(b)

    one = jnp.where(first, 1.0, 0.0).astype(F32)
    mats, ets, advs = [], [], []
    for d in range(2):
        lam = lam_ref[0, d:d + 1, :]
        z = lam * jnp.exp(step_ref[0, d:d + 1, :])
        ang = dupi(z)
        lbar = jnp.exp(dupr(z)) * jnp.where(first, jnp.cos(ang), jnp.sin(ang))
        den = dupr(lam) * dupr(lam) + dupi(lam) * dupi(lam)
        qcoef = cmul(lbar - one, lam * conj_sgn) / den
        bbar = cmul(qcoef, bt_ref[0])
        cmat = c_ref[0, d * H:(d + 1) * H, :]
        pw = [one]
        for _ in range(L):
            pw.append(cmul(pw[-1], lbar))
        cpow = [cmul(pw[t], cmat) * conj_sgn for t in range(L + 1)]
        if d == 0:
            kern_rows = jnp.concatenate([cpow[t] for t in range(L)], axis=0)
            inj = jnp.concatenate([cmul(pw[L - 1 - j], bbar) for j in range(L)], axis=0)
            read = jnp.concatenate([cpow[t + 1] for t in range(L)], axis=0)
        else:
            kern_rows = jnp.concatenate([cpow[L - 1 - t] for t in range(L)], axis=0)
            inj = jnp.concatenate([cmul(pw[j], bbar) for j in range(L)], axis=0)
            read = jnp.concatenate([cpow[L - t] for t in range(L)], axis=0)
        base = lax.dot_general(bbar.astype(BF16), kern_rows.astype(BF16), (((1,), (1,)), ((), ())),
                               preferred_element_type=F32)
        lane_o = lax.broadcasted_iota(jnp.int32, (H, L * H), 1)
        blocks = []
        for j in range(L):
            if d == 0:
                sh = (j * H) % (L * H)
                blk = jnp.where(lane_o >= j * H, pltpu.roll(base, sh, axis=1) if sh else base, 0.0)
            else:
                sh = (L * H - (L - 1 - j) * H) % (L * H)
                blk = jnp.where(lane_o < (j + 1) * H, pltpu.roll(base, sh, axis=1) if sh else base, 0.0)
            blocks.append(blk)
        mats.append((jnp.concatenate(blocks, axis=0), inj))
        ets.append(read)
        pl_ = pw[L]
        advs.append((dupr(pl_), sgn * dupi(pl_)))
    mix = (mats[0][0] + mats[1][0]).astype(BF16)
    rhs = jnp.concatenate([mix, mats[0][1].astype(BF16), mats[1][1].astype(BF16)], axis=1)
    yz_ref[0] = jnp.dot(u_ref[0], rhs, preferred_element_type=F32)
    et_ref[0] = jnp.concatenate([ets[0], ets[1]], axis=1).astype(BF16)
    adv_ref[0] = jnp.concatenate([jnp.concatenate([advs[0][0], advs[1][0]], axis=1),
                                  jnp.concatenate([advs[0][1], advs[1][1]], axis=1)], axis=0)


def _s5_scan_kernel(z_ref, adv_ref, s_ref, *, n_chunks, nc_ctx):
    P = S5_STATE
    a1 = adv_ref[:, 0, :]
    a2 = adv_ref[:, 1, :]
    gb = a1.shape[0]
    nl = n_chunks - nc_ctx

    def swap_halves(x):
        return jnp.concatenate([pltpu.roll(x[:, :2 * P], P, axis=1), pltpu.roll(x[:, 2 * P:], P, axis=1)], axis=1)

    def body(i, carry):
        s, ssw = carry
        cf = jnp.where(i < nc_ctx, nl + i, i - nc_ctx)
        cb = n_chunks - 1 - i
        z = jnp.concatenate([z_ref[cf][:, :2 * P], z_ref[cb][:, 2 * P:]], axis=1)
        s_ref[cf, :, 0:2 * P] = s[:, :2 * P].astype(s_ref.dtype)
        s_ref[cb, :, 2 * P:4 * P] = s[:, 2 * P:].astype(s_ref.dtype)
        return a1 * s + a2 * ssw + z, a1 * ssw - a2 * s + swap_halves(z)

    zero = jnp.zeros((gb, 4 * P), F32)
    lax.fori_loop(0, n_chunks, body, (zero, zero), unroll=4)


def _s5_out_kernel(yz_ref, s_ref, et_ref, o_ref):
    n = o_ref.shape[2]
    o_ref[0] = yz_ref[0, :, 0:n] + lax.dot_general(s_ref[0], et_ref[0], (((1,), (1,)), ((), ())),
                                                  preferred_element_type=F32)


def _s5_finish_kernel(y_ref, u_ref, d_ref, w_ref, b_ref, o_ref):
    y = y_ref[...] + d_ref[...] * u_ref[...]
    g = 0.5 * y * (1.0 + jnp.tanh(math.sqrt(2.0 / math.pi) * (y + 0.044715 * (y * y * y))))
    gate = jnp.dot(g.astype(BF16), w_ref[...], preferred_element_type=F32) + b_ref[...]
    o_ref[...] = (g * _sigmoid(gate)).astype(BF16)


def _perm_kernel(x_ref, w_ref, o_ref):
    x = x_ref[...]
    w = w_ref[...]
    hi = x.astype(BF16)
    acc = jnp.dot(hi, w, preferred_element_type=F32)
    if x.dtype == F32:
        lo = (x - hi.astype(F32)).astype(BF16)
        acc = acc + jnp.dot(lo, w, preferred_element_type=F32)
    o_ref[...] = acc.astype(o_ref.dtype)


def _permute_lanes(x, pmat, out_dtype):
    r, n = x.shape
    bm = _pick(r, (1408, 768, 512, 256, 128, 64, 32, 16, 8))
    bn = _pick(n, (512, 256, 128))
    return pl.pallas_call(
        _perm_kernel,
        grid=(r // bm, n // bn),
        in_specs=[pl.BlockSpec((bm, n), lambda i, j: (i, 0)), pl.BlockSpec((n, bn), lambda i, j: (0, j))],
        out_specs=pl.BlockSpec((bm, bn), lambda i, j: (i, j)),
        out_shape=jax.ShapeDtypeStruct((r, n), out_dtype),
        compiler_params=_cparams(("parallel", "arbitrary"),
                                 _vmem([((bm, n), x.dtype), ((n, bn), BF16), ((bm, bn), F32), ((bm, n), BF16)])),
        name="lane_permute",
    )(x, pmat)


def _s5_branch(u, lam_re, lam_im, log_step, b_re, b_im, c_re, c_im, d_skip, glu_w, glu_b, n_lat):
    t = u.shape[0]
    G, H, P, L = S5_GROUPS, S5_GROUP, S5_STATE, S5_CHUNK
    nch = t // L
    nc_ctx = (t - n_lat) // L
    ntile, gpt = (G * H) // LANES, LANES // H
    src = jnp.arange(L * LANES)
    dst = ((src % LANES) // H) * (L * H) + (src // LANES) * H + src % H
    to_group = (dst[:, None] == src[None, :]).astype(BF16)
    to_token = (src[:, None] == dst[None, :]).astype(BF16)
    lam_p = jnp.concatenate([lam_re, lam_im], axis=-1).transpose(1, 0, 2)
    step_p = jnp.broadcast_to(log_step.T[:, :, None], (G, 2, 2 * P))
    bt_p = jnp.concatenate([b_re, b_im], axis=1).transpose(0, 2, 1)
    c_p = jnp.concatenate([c_re, c_im], axis=-1).transpose(1, 0, 2, 3).reshape(G, 2 * H, 2 * P)
    LH = L * H
    ut = u.astype(BF16).reshape(nch, L, ntile, LANES).transpose(2, 0, 1, 3).reshape(ntile * nch, L * LANES)
    ug = (_permute_lanes(ut, to_group, BF16).reshape(ntile, nch, gpt, LH).transpose(0, 2, 1, 3)
          .reshape(G, nch, LH))
    yz, et, adv = pl.pallas_call(
        _s5_build_kernel,
        grid=(G,),
        in_specs=[pl.BlockSpec((1, nch, LH), lambda g: (g, 0, 0)),
                  pl.BlockSpec((1, 2, 2 * P), lambda g: (g, 0, 0)),
                  pl.BlockSpec((1, 2, 2 * P), lambda g: (g, 0, 0)),
                  pl.BlockSpec((1, H, 2 * P), lambda g: (g, 0, 0)),
                  pl.BlockSpec((1, 2 * H, 2 * P), lambda g: (g, 0, 0))],
        out_specs=[pl.BlockSpec((1, nch, LH + 4 * P), lambda g: (g, 0, 0)),
                   pl.BlockSpec((1, LH, 4 * P), lambda g: (g, 0, 0)),
                   pl.BlockSpec((1, 2, 4 * P), lambda g: (g, 0, 0))],
        out_shape=[jax.ShapeDtypeStruct((G, nch, LH + 4 * P), F32),
                   jax.ShapeDtypeStruct((G, LH, 4 * P), BF16),
                   jax.ShapeDtypeStruct((G, 2, 4 * P), F32)],
        compiler_params=_cparams(("parallel",), _vmem([((nch, LH), BF16), ((nch, LH + 4 * P), F32)])),
        name="s5_build_apply",
    )(ug, lam_p, step_p, bt_p, c_p)
    z_t = yz[:, :, LH:].transpose(1, 0, 2)
    gb = _pick(G, (16, 8))
    s_in = pl.pallas_call(
        functools.partial(_s5_scan_kernel, n_chunks=nch, nc_ctx=nc_ctx),
        grid=(G // gb,),
        in_specs=[pl.BlockSpec((nch, gb, 4 * P), lambda g: (0, g, 0)),
                  pl.BlockSpec((gb, 2, 4 * P), lambda g: (g, 0, 0))],
        out_specs=pl.BlockSpec((nch, gb, 4 * P), lambda g: (0, g, 0)),
        out_shape=jax.ShapeDtypeStruct((nch, G, 4 * P), BF16),
        compiler_params=_cparams(("parallel",), _vmem([((nch, gb, 4 * P), F32), ((nch, gb, 4 * P), BF16)])),
        name="s5_chunk_scan",
    )(z_t, adv)
    s_g = s_in.transpose(1, 0, 2)
    y = pl.pallas_call(
        _s5_out_kernel,
        grid=(G,),
        in_specs=[pl.BlockSpec((1, nch, LH + 4 * P), lambda g: (g, 0, 0)),
                  pl.BlockSpec((1, nch, 4 * P), lambda g: (g, 0, 0)),
                  pl.BlockSpec((1, LH, 4 * P), lambda g: (g, 0, 0))],
        out_specs=pl.BlockSpec((1, nch, LH), lambda g: (g, 0, 0)),
        out_shape=jax.ShapeDtypeStruct((G, nch, LH), F32),
        compiler_params=_cparams(("parallel",), _vmem([((nch, LH + 4 * P), F32), ((nch, 4 * P), BF16),
                                                       ((nch, LH), F32)])),
        name="s5_readout",
    )(yz, s_g, et)
    yt = y.reshape(ntile, gpt, nch, LH).transpose(0, 2, 1, 3).reshape(ntile * nch, gpt * LH)
    y_tok = (_permute_lanes(yt, to_token, F32).reshape(ntile, nch, L, LANES).transpose(1, 2, 0, 3)
             .reshape(t, G * H))
    w = G * H
    bm = _pick(t, (256, 128))
    return pl.pallas_call(
        _s5_finish_kernel,
        grid=(t // bm,),
        in_specs=[pl.BlockSpec((bm, w), lambda i: (i, 0)),
                  pl.BlockSpec((bm, w), lambda i: (i, 0)),
                  pl.BlockSpec((1, w), lambda i: (0, 0)),
                  pl.BlockSpec((w, w), lambda i: (0, 0)),
                  pl.BlockSpec((1, w), lambda i: (0, 0))],
        out_specs=pl.BlockSpec((bm, w), lambda i: (i, 0)),
        out_shape=jax.ShapeDtypeStruct((t, w), BF16),
        compiler_params=_cparams(("parallel",), _vmem([((bm, w), F32), ((bm, w), F32), ((w, w), BF16),
                                                       ((bm, w), F32)])),
        name="s5_gelu_glu",
    )(y_tok, u, d_skip.reshape(1, w), glu_w.astype(BF16), glu_b.reshape(1, w))


def _ssd_chunk(forward, xs_ref, b_ref, c_ref, dt_ref, sel_ref, bias_ref, alog_ref, y_ref, state_ref, hpg):
    L = SSD_CHUNK
    hd = M2_HEAD_DIM
    sel = sel_ref[0]
    hi, mid, lo = _split3(dt_ref[...])
    dt_raw = (jnp.dot(hi, sel, preferred_element_type=F32) + jnp.dot(mid, sel, preferred_element_type=F32)
              + jnp.dot(lo, sel, preferred_element_type=F32))
    v = dt_raw + bias_ref[0]
    dt = jnp.maximum(v, 0.0) + jnp.log1p(jnp.exp(-jnp.abs(v)))
    da = dt * (-jnp.exp(alog_ref[0]))
    ti = lax.broadcasted_iota(jnp.int32, (L, L), 0)
    si = lax.broadcasted_iota(jnp.int32, (L, L), 1)
    tri = jnp.where(si <= ti, 1.0, 0.0).astype(BF16)
    hi, mid, lo = _split3(da)
    cum = (jnp.dot(tri, hi, preferred_element_type=F32) + jnp.dot(tri, mid, preferred_element_type=F32)
           + jnp.dot(tri, lo, preferred_element_type=F32))
    total = cum[L - 1:L, :]
    if forward:
        cumq = cum
        w_out = jnp.exp(cumq)
        w_state = jnp.exp(total - cumq)
        mask = si <= ti
    else:
        cumq = cum - da
        w_out = jnp.exp(total - cumq)
        w_state = jnp.exp(cumq)
        mask = si >= ti
    xsdt = xs_ref[...] * dt
    cb_ = c_ref[...]
    bb_ = b_ref[...]
    s_old = state_ref[...]
    y_off = w_out * jnp.dot(cb_, s_old.astype(BF16), preferred_element_type=F32)
    state_ref[...] = jnp.exp(total) * s_old + lax.dot_general(
        bb_, (w_state * xsdt).astype(BF16), (((0,), (0,)), ((), ())), preferred_element_type=F32)
    cbm = lax.dot_general(cb_, bb_, (((1,), (1,)), ((), ())), preferred_element_type=F32)
    cum_t = cumq.T
    lane = lax.broadcasted_iota(jnp.int32, (L, LANES), 1)
    xb = xsdt.astype(BF16)
    zero = jnp.zeros((L, LANES), BF16)
    outs = []
    for pr in range(hpg // 2):
        gs = []
        for r in (2 * pr, 2 * pr + 1):
            col = cumq[:, hd * r:hd * r + 1]
            row = cum_t[hd * r:hd * r + 1, :]
            diff = (col - row) if forward else (row - col)
            decay = jnp.exp(jnp.where(mask, diff, NEG_BIG))
            gs.append((cbm * decay).astype(BF16))
        xp = xb[:, LANES * pr:LANES * (pr + 1)]
        rhs = jnp.concatenate([jnp.where(lane < hd, xp, zero), jnp.where(lane >= hd, xp, zero)], axis=0)
        outs.append(jnp.dot(jnp.concatenate(gs, axis=1), rhs, preferred_element_type=F32))
    y_ref[...] = jnp.concatenate(outs, axis=1) + y_off


def _ssd_kernel(*refs, hpg, gpb):
    fwd, bwd, (yf_ref, yb_ref, sf_ref, sb_ref) = refs[0:7], refs[7:14], refs[14:18]
    w = hpg * M2_HEAD_DIM

    @pl.when(pl.program_id(1) == 0)
    def _():
        sf_ref[...] = jnp.zeros_like(sf_ref)
        sb_ref[...] = jnp.zeros_like(sb_ref)

    for forward, ins, y_ref, s_ref in ((True, fwd, yf_ref, sf_ref), (False, bwd, yb_ref, sb_ref)):
        xs_ref, b_ref, c_ref, dt_ref, sel_ref, bias_ref, alog_ref = ins
        for k in range(gpb):
            cols = slice(k * w, (k + 1) * w)
            st = slice(k * M2_STATE, (k + 1) * M2_STATE)
            _ssd_chunk(forward, xs_ref.at[:, cols], b_ref.at[:, st], c_ref.at[:, st], dt_ref,
                       sel_ref.at[k:k + 1], bias_ref.at[k:k + 1], alog_ref.at[k:k + 1],
                       y_ref.at[:, cols], s_ref.at[k], hpg)


def _ssd_finish_kernel(yf_ref, yb_ref, xs_ref, z_ref, d_ref, nw_ref, o_ref, *, n_groups):
    y = yf_ref[...] + yb_ref[...] + d_ref[...] * xs_ref[...]
    z = z_ref[...]
    y = y * (z * _sigmoid(z))
    gw = y.shape[1] // n_groups
    parts = []
    for g in range(n_groups):
        seg = y[:, g * gw:(g + 1) * gw]
        parts.append(seg * lax.rsqrt(jnp.mean(seg * seg, axis=-1, keepdims=True) + RMS_EPS))
    o_ref[...] = (jnp.concatenate(parts, axis=1) * nw_ref[...]).astype(BF16)


def _ssd_branch(xs, bm_, cm_, dt_raw, z, a_log, dt_bias, d_skip, norm_w, n_lat):
    t, inner = xs.shape
    L = SSD_CHUNK
    ng = M2_GROUPS
    hpg = M2_HEADS // ng
    w = hpg * M2_HEAD_DIM
    nch = t // L
    nc_ctx = (t - n_lat) // L
    nl = nch - nc_ctx
    src = (jnp.arange(2)[:, None, None] * M2_HEADS + jnp.arange(ng)[None, :, None] * hpg
           + jnp.arange(w)[None, None, :] // M2_HEAD_DIM)
    sel = (jnp.arange(LANES)[None, None, :, None] == src[:, :, None, :]).astype(BF16)
    expand = lambda p: jnp.repeat(p, M2_HEAD_DIM, axis=-1).reshape(2, ng, 1, w)
    bias_e = expand(dt_bias)
    alog_e = expand(a_log)

    order = (lambda c: jnp.where(c < nc_ctx, nl + c, c - nc_ctx), lambda c: nch - 1 - c)
    gpb = 2 if ng % 2 == 0 else 1
    in_specs, args = [], []
    for d in range(2):
        ch = order[d]
        in_specs += [pl.BlockSpec((L, gpb * w), lambda g, c, ch=ch: (ch(c), g)),
                     pl.BlockSpec((L, gpb * M2_STATE), lambda g, c, ch=ch: (ch(c), g)),
                     pl.BlockSpec((L, gpb * M2_STATE), lambda g, c, ch=ch: (ch(c), g)),
                     pl.BlockSpec((L, LANES), lambda g, c, ch=ch: (ch(c), 0)),
                     pl.BlockSpec((gpb, LANES, w), lambda g, c: (g, 0, 0)),
                     pl.BlockSpec((gpb, 1, w), lambda g, c: (g, 0, 0)),
                     pl.BlockSpec((gpb, 1, w), lambda g, c: (g, 0, 0))]
        args += [xs, bm_, cm_, dt_raw, sel[d], bias_e[d], alog_e[d]]
    yf, yb = pl.pallas_call(
        functools.partial(_ssd_kernel, hpg=hpg, gpb=gpb),
        grid=(ng // gpb, nch),
        in_specs=in_specs,
        out_specs=[pl.BlockSpec((L, gpb * w), lambda g, c: (order[0](c), g)),
                   pl.BlockSpec((L, gpb * w), lambda g, c: (order[1](c), g))],
        out_shape=[jax.ShapeDtypeStruct((t, inner), F32), jax.ShapeDtypeStruct((t, inner), F32)],
        scratch_shapes=[pltpu.VMEM((gpb, M2_STATE, w), F32), pltpu.VMEM((gpb, M2_STATE, w), F32)],
        compiler_params=_cparams(("parallel", "arbitrary"), 32 << 20),
        name="ssd_chunk_scan",
    )(*args)
    bm = _pick(t, (256, 128))
    d_e = jnp.repeat(d_skip, M2_HEAD_DIM).reshape(1, inner)
    return pl.pallas_call(
        functools.partial(_ssd_finish_kernel, n_groups=ng),
        grid=(t // bm,),
        in_specs=[pl.BlockSpec((bm, inner), lambda i: (i, 0)),
                  pl.BlockSpec((bm, inner), lambda i: (i, 0)),
                  pl.BlockSpec((bm, inner), lambda i: (i, 0)),
                  pl.BlockSpec((bm, inner), lambda i: (i, 0)),
                  pl.BlockSpec((1, inner), lambda i: (0, 0)),
                  pl.BlockSpec((1, inner), lambda i: (0, 0))],
        out_specs=pl.BlockSpec((bm, inner), lambda i: (i, 0)),
        out_shape=jax.ShapeDtypeStruct((t, inner), BF16),
        compiler_params=_cparams(("parallel",), _vmem([((bm, inner), F32)] * 5)),
        name="ssd_gate_norm",
    )(yf, yb, xs, z, d_e, norm_w.reshape(1, inner))


def _rope_tables(n_lat, n_ctx):
    n_rows = n_lat // GRID_W
    rows = jnp.repeat(jnp.arange(n_rows), GRID_W).astype(F32)
    cols = jnp.tile(jnp.arange(GRID_W), n_rows).astype(F32)
    inv = jnp.power(ROPE_BASE, -jnp.arange(0, ROPE_AXIS_DIM, 2, dtype=F32) / ROPE_AXIS_DIM)
    ang_r = rows[:, None] * inv
    ang_c = cols[:, None] * inv
    ang = jnp.concatenate([ang_r, ang_r, ang_c, ang_c], axis=-1)
    cos, sin = jnp.cos(ang), jnp.sin(ang)
    half = ROPE_AXIS_DIM // 2
    first = (jnp.arange(A_HEAD_DIM) % ROPE_AXIS_DIM) < half
    s1 = jnp.where(first, 0.0, sin)
    s2 = jnp.where(first, -sin, 0.0)
    pad = lambda tbl, fill: jnp.concatenate([tbl, jnp.full((n_ctx, A_HEAD_DIM), fill, F32)], axis=0)
    two = lambda tbl: jnp.concatenate([tbl, tbl], axis=-1)
    return two(pad(cos, 1.0)), two(pad(s1, 0.0)), two(pad(s2, 0.0))


def _swiglu_ffn(h, mods, nw, w1_all, w3_all, layer, w2, n_lat):
    a = _normmod(h, nw, mods, 3, 4, n_lat)
    hid = _mm_pair(_swiglu_up_kernel, a, w1_all, w3_all, BF16, layer=layer, name="ffn_gate_up")
    return _mm_resid([hid], [w2], h, mods, 5, n_lat, bn_cands=(256, 128), layer=layer, name="ffn_down_resid")


def _even_layer(h, a, mods, w_in_all, j, w_out, lam, subw, conv_w, conv_b, ln_w, ln_b, rope, layer_idx, n_lat):
    n_ctx = h.shape[0] - n_lat
    kvq = _qkv_proj(a, w_in_all, j, *rope)
    glu = _mm_pair(_glu_kernel, a, w_in_all, w_in_all, F32, off_a=EV_B0, off_g=EV_B0 + B_WIDTH, n_cols=B_WIDTH,
                   layer=j, name="conv_glu_proj")
    lam_init = 0.8 - 0.6 * math.exp(-0.3 * layer_idx)
    att = _attention(kvq, lam, subw, n_lat, n_ctx, lam_init)
    cv = _dwconv_silu(glu, conv_w, conv_b, n_lat, BF16, ln=(ln_w, ln_b))
    w_out = w_out.astype(BF16)
    return _mm_resid([att, cv], [w_out[:A_WIDTH], w_out[A_WIDTH:]], h, mods, 2, n_lat, name="even_out_resid")


def _odd_layer(h, a, mods, w_in_all, j, w_out, s5p, m2p, n_lat):
    u = _mm_plain(a, w_in_all, F32, col_off=0, n_cols=S5_WIDTH, layer=j)
    xbc = _mm_plain(a, w_in_all, F32, col_off=S5_WIDTH, n_cols=M2_CONV_DIM, layer=j)
    n_dt = 2 * M2_HEADS
    k_in, n_in = w_in_all.shape[1:]
    w_dt = lax.slice(w_in_all, (j, 0, OD_DT0), (j + 1, k_in, OD_Z0))[0]
    w_dt = jnp.pad(w_dt, ((0, 0), (0, LANES - n_dt))).astype(BF16)
    dt_raw = _mm_plain(a, w_dt, F32)
    z = _mm_plain(a, lax.slice(w_in_all, (j, 0, OD_Z0), (j + 1, k_in, n_in))[0].astype(BF16), F32)
    s5_out = _s5_branch(u, *s5p, n_lat)
    conv_w, conv_b, a_log, dt_bias, m2_d, m2_norm_w = m2p
    gn = M2_GROUPS * M2_STATE
    xs = _dwconv_silu(xbc, conv_w[:, :M2_INNER], conv_b[:M2_INNER], n_lat, F32, col_off=0, n_cols=M2_INNER)
    bmat = _dwconv_silu(xbc, conv_w[:, M2_INNER:M2_INNER + gn], conv_b[M2_INNER:M2_INNER + gn], n_lat, BF16,
                        col_off=M2_INNER, n_cols=gn)
    cmat = _dwconv_silu(xbc, conv_w[:, M2_INNER + gn:], conv_b[M2_INNER + gn:], n_lat, BF16,
                        col_off=M2_INNER + gn, n_cols=gn)
    ssd_out = _ssd_branch(xs, bmat, cmat, dt_raw, z, a_log, dt_bias, m2_d, m2_norm_w, n_lat)
    w_out = w_out.astype(BF16)
    return _mm_resid([s5_out, ssd_out], [w_out[:S5_WIDTH], w_out[S5_WIDTH:]], h, mods, 2, n_lat,
                     name="odd_out_resid")


def kernel(x, c, ctx, c_ctx, ada_w, ada_b, norm_w, ffn_w1, ffn_w3, ffn_w2, ev_w_in, ev_w_out, ev_lambda, ev_subln_w, ev_conv_w, ev_conv_b, ev_ln_w, ev_ln_b, od_w_in, od_w_out, s5_lam_re, s5_lam_im, s5_log_step, s5_b_re, s5_b_im, s5_c_re, s5_c_im, s5_d, s5_glu_w, s5_glu_b, m2_conv_w, m2_conv_b, m2_a_log, m2_dt_bias, m2_d, m2_norm_w, final_norm_w):
    assert x.shape[0] == 1 and c.shape[0] == 1 and ctx.shape[0] == 1
    n_lat, n_ctx = x.shape[1], ctx.shape[1]
    depth = ada_w.shape[0]
    mods_all = _ada_mods(c, c_ctx, ada_w, ada_b)
    rope = _rope_tables(n_lat, n_ctx)
    w2_all = ffn_w2.astype(BF16)
    h = None
    for i in range(depth):
        mods = mods_all[i]
        j = i // 2
        if i == 0:
            a, h = _normmod_first(x[0], ctx[0], norm_w[i, 0], mods, 0, 1)
        else:
            a = _normmod(h, norm_w[i, 0], mods, 0, 1, n_lat)
        if i % 2 == 0:
            h = _even_layer(h, a, mods, ev_w_in, j, ev_w_out[j], ev_lambda[j], ev_subln_w[j],
                            ev_conv_w[j], ev_conv_b[j], ev_ln_w[j], ev_ln_b[j], rope, i, n_lat)
        else:
            s5p = (s5_lam_re[j], s5_lam_im[j], s5_log_step[j], s5_b_re[j], s5_b_im[j], s5_c_re[j], s5_c_im[j],
                   s5_d[j], s5_glu_w[j], s5_glu_b[j])
            m2p = (m2_conv_w[j], m2_conv_b[j], m2_a_log[j], m2_dt_bias[j], m2_d[j], m2_norm_w[j])
            h = _odd_layer(h, a, mods, od_w_in, j, od_w_out[j], s5p, m2p, n_lat)
        h = _swiglu_ffn(h, mods, norm_w[i, 1], ffn_w1, ffn_w3, i, w2_all, n_lat)
    return _final_norm(h, final_norm_w, n_lat)[None]
```

```python
import functools
import math

import jax
import jax.numpy as jnp
from jax import lax
from jax.experimental import pallas as pl
from jax.experimental.pallas import tpu as pltpu

F32 = jnp.float32
BF16 = jnp.bfloat16

D_MODEL = 4096
SEQ = 8192
DEPTH = 2
CTX_LEN = 256
GRID_W = 64
RMS_EPS = 1e-6
LN_EPS = 1e-5
A_WIDTH = D_MODEL // 2
A_HEAD_DIM = 64
A_HEADS = A_WIDTH // (2 * A_HEAD_DIM)
B_WIDTH = D_MODEL - A_WIDTH
CONV_WIDTH = 31
ROPE_BASE = 10000.0
ROPE_AXIS_DIM = A_HEAD_DIM // 2
EV_V0 = A_WIDTH
EV_Q0 = 2 * A_WIDTH
EV_B0 = 3 * A_WIDTH
S5_WIDTH = D_MODEL // 4
S5_GROUP = 16
S5_GROUPS = S5_WIDTH // S5_GROUP
S5_STATE = 64
M2_INNER = D_MODEL - S5_WIDTH
M2_HEAD_DIM = 64
M2_HEADS = M2_INNER // M2_HEAD_DIM
M2_GROUPS = 8
M2_STATE = 128
M2_CONV = 5
M2_CONV_DIM = M2_INNER + 2 * M2_GROUPS * M2_STATE
OD_DT0 = S5_WIDTH + M2_CONV_DIM
OD_Z0 = OD_DT0 + 2 * M2_HEADS

V7X_VMEM_BYTES = 64 * 1024 * 1024
LANES = 128
SUBLANES = 8
HALO = 16
S5_CHUNK = 16
SSD_CHUNK = 128
NEG_BIG = -1e30


def _cparams(sem, vmem_bytes):
    return pltpu.CompilerParams(dimension_semantics=sem,
                                vmem_limit_bytes=int(min(vmem_bytes, V7X_VMEM_BYTES - (4 << 20))))


def _pick(n, cands):
    for c in cands:
        if n % c == 0:
            return c
    raise ValueError(f"no block size in {cands} divides {n}")


def _nbytes(shape, dtype):
    return math.prod(shape) * jnp.dtype(dtype).itemsize


def _vmem(blocks, scratch=0):
    return 2 * sum(_nbytes(s, d) for s, d in blocks) + scratch + (12 << 20)


def _sigmoid(x):
    return jax.nn.sigmoid(x)


def _split3(x):
    hi = x.astype(BF16)
    r1 = x - hi.astype(F32)
    mid = r1.astype(BF16)
    lo = (r1 - mid.astype(F32)).astype(BF16)
    return hi, mid, lo


def _row_is_lat(i, bm, n_lat):
    row = i * bm + lax.broadcasted_iota(jnp.int32, (bm, 1), 0)
    return row < n_lat


def _ada_kernel(x_ref, w_ref, b_ref, o_ref):
    x = x_ref[...]
    xs = (x * _sigmoid(x)).astype(BF16)
    o_ref[0] = jnp.dot(xs, w_ref[0].astype(BF16), preferred_element_type=F32) + b_ref[0]


def _ada_mods(c, c_ctx, ada_w, ada_b):
    depth, d, n6 = ada_w.shape
    xin = jnp.zeros((SUBLANES, d), F32).at[0].set(c[0]).at[1].set(c_ctx)
    bn = _pick(n6, (512, 256, 128))
    return pl.pallas_call(
        _ada_kernel,
        grid=(depth, n6 // bn),
        in_specs=[pl.BlockSpec((SUBLANES, d), lambda l, j: (0, 0)),
                  pl.BlockSpec((1, d, bn), lambda l, j: (l, 0, j)),
                  pl.BlockSpec((1, 1, bn), lambda l, j: (l, 0, j))],
        out_specs=pl.BlockSpec((1, SUBLANES, bn), lambda l, j: (l, 0, j)),
        out_shape=jax.ShapeDtypeStruct((depth, SUBLANES, n6), F32),
        compiler_params=_cparams(("parallel", "parallel"), _vmem([((d, bn), F32), ((d, bn), BF16)])),
        name="ada_mods",
    )(xin, ada_w, ada_b.reshape(depth, 1, n6))


def _normmod_kernel(h_ref, nw_ref, sh_ref, sc_ref, o_ref, *, bm, n_lat):
    x = h_ref[...]
    y = x * lax.rsqrt(jnp.mean(x * x, axis=-1, keepdims=True) + RMS_EPS) * nw_ref[...]
    is_lat = _row_is_lat(pl.program_id(0), bm, n_lat)
    sh = jnp.where(is_lat, sh_ref[0:1, :], sh_ref[1:2, :])
    sc = jnp.where(is_lat, sc_ref[0:1, :], sc_ref[1:2, :])
    o_ref[...] = (y * (1.0 + sc) + sh).astype(BF16)


def _normmod(h, nw, mods, k_shift, k_scale, n_lat):
    t, d = h.shape
    bm = _pick(t, (384, 256, 128))
    return pl.pallas_call(
        functools.partial(_normmod_kernel, bm=bm, n_lat=n_lat),
        grid=(t // bm,),
        in_specs=[pl.BlockSpec((bm, d), lambda i: (i, 0)),
                  pl.BlockSpec((1, d), lambda i: (0, 0)),
                  pl.BlockSpec((SUBLANES, d), lambda i: (0, k_shift)),
                  pl.BlockSpec((SUBLANES, d), lambda i: (0, k_scale))],
        out_specs=pl.BlockSpec((bm, d), lambda i: (i, 0)),
        out_shape=jax.ShapeDtypeStruct((t, d), BF16),
        compiler_params=_cparams(("parallel",), _vmem([((bm, d), F32), ((bm, d), BF16), ((bm, d), F32)])),
        name="norm_modulate",
    )(h, nw.reshape(1, d), mods, mods)


def _normmod_first_kernel(x_ref, ctx_ref, nw_ref, sh_ref, sc_ref, o_ref, h_ref, *, nb_lat):
    is_lat = pl.program_id(0) < nb_lat
    x = jnp.where(is_lat, x_ref[...], ctx_ref[...])
    h_ref[...] = x
    y = x * lax.rsqrt(jnp.mean(x * x, axis=-1, keepdims=True) + RMS_EPS) * nw_ref[...]
    sh = jnp.where(is_lat, sh_ref[0:1, :], sh_ref[1:2, :])
    sc = jnp.where(is_lat, sc_ref[0:1, :], sc_ref[1:2, :])
    o_ref[...] = (y * (1.0 + sc) + sh).astype(BF16)


def _normmod_first(x, ctx, nw, mods, k_shift, k_scale):
    (n_lat, d), n_ctx = x.shape, ctx.shape[0]
    t = n_lat + n_ctx
    bm = _pick(math.gcd(n_lat, n_ctx), (256, 128))
    nb_lat = n_lat // bm
    return pl.pallas_call(
        functools.partial(_normmod_first_kernel, nb_lat=nb_lat),
        grid=(t // bm,),
        in_specs=[pl.BlockSpec((bm, d), lambda i: (jnp.minimum(i, nb_lat - 1), 0)),
                  pl.BlockSpec((bm, d), lambda i: (jnp.maximum(i - nb_lat, 0), 0)),
                  pl.BlockSpec((1, d), lambda i: (0, 0)),
                  pl.BlockSpec((SUBLANES, d), lambda i: (0, k_shift)),
                  pl.BlockSpec((SUBLANES, d), lambda i: (0, k_scale))],
        out_specs=[pl.BlockSpec((bm, d), lambda i: (i, 0)), pl.BlockSpec((bm, d), lambda i: (i, 0))],
        out_shape=[jax.ShapeDtypeStruct((t, d), BF16), jax.ShapeDtypeStruct((t, d), F32)],
        compiler_params=_cparams(("parallel",), _vmem([((bm, d), F32)] * 4)),
        name="norm_modulate_first",
    )(x, ctx, nw.reshape(1, d), mods, mods)


def _final_norm_kernel(h_ref, nw_ref, o_ref):
    x = h_ref[...]
    o_ref[...] = x * lax.rsqrt(jnp.mean(x * x, axis=-1, keepdims=True) + RMS_EPS) * nw_ref[...]


def _final_norm(h, nw, n_lat):
    t, d = h.shape
    bm = _pick(n_lat, (256, 128))
    return pl.pallas_call(
        _final_norm_kernel,
        grid=(n_lat // bm,),
        in_specs=[pl.BlockSpec((bm, d), lambda i: (i, 0)),
                  pl.BlockSpec((1, d), lambda i: (0, 0))],
        out_specs=pl.BlockSpec((bm, d), lambda i: (i, 0)),
        out_shape=jax.ShapeDtypeStruct((n_lat, d), F32),
        compiler_params=_cparams(("parallel",), _vmem([((bm, d), F32), ((bm, d), F32), ((bm, d), F32)])),
        name="final_norm",
    )(h, nw.reshape(1, d))


def _row_block(t, w):
    return _pick(t, (1408, 768, 512, 256, 128) if w.dtype == F32 else (768, 512, 256, 128))


def _col_cands(w):
    return (256, 128) if w.dtype == F32 else (512, 256, 128)


def _wspec(w, layer, bn, ob):
    k = w.shape[-2]
    if w.ndim == 3:
        return pl.BlockSpec((None, k, bn), lambda i, j: (layer, 0, j + ob))
    return pl.BlockSpec((k, bn), lambda i, j: (0, j + ob))


def _mm_plain_kernel(x_ref, w_ref, o_ref):
    o_ref[...] = jnp.dot(x_ref[...], w_ref[...].astype(BF16), preferred_element_type=F32).astype(o_ref.dtype)


def _mm_plain(x, w, out_dtype, *, col_off=0, n_cols=None, layer=0):
    t, k = x.shape
    n_cols = w.shape[-1] - col_off if n_cols is None else n_cols
    bm = _row_block(t, w)
    bn = _pick(math.gcd(n_cols, col_off) if col_off else n_cols, _col_cands(w))
    ob = col_off // bn
    return pl.pallas_call(
        _mm_plain_kernel,
        grid=(t // bm, n_cols // bn),
        in_specs=[pl.BlockSpec((bm, k), lambda i, j: (i, 0)), _wspec(w, layer, bn, ob)],
        out_specs=pl.BlockSpec((bm, bn), lambda i, j: (i, j)),
        out_shape=jax.ShapeDtypeStruct((t, n_cols), out_dtype),
        compiler_params=_cparams(("parallel", "arbitrary"),
                                 _vmem([((bm, k), BF16), ((k, bn), w.dtype), ((bm, bn), F32)])),
        name="matmul",
    )(x, w)


def _qkv_kernel(x_ref, w_ref, cos_ref, s1_ref, s2_ref, o_ref, *, nb_region, bn, q_scale):
    acc = jnp.dot(x_ref[...], w_ref[...].astype(BF16), preferred_element_type=F32)
    region = pl.program_id(1) // nb_region

    @pl.when(region == 1)
    def _():
        o_ref[...] = acc.astype(BF16)

    @pl.when(region != 1)
    def _():
        reps = bn // LANES
        cos = jnp.tile(cos_ref[...], (1, reps))
        s1 = jnp.tile(s1_ref[...], (1, reps))
        s2 = jnp.tile(s2_ref[...], (1, reps))
        out = acc * cos + pltpu.roll(acc, 16, axis=1) * s1 + pltpu.roll(acc, bn - 16, axis=1) * s2
        scale = jnp.where(region == 2, q_scale, 1.0).astype(F32)
        o_ref[...] = (out * scale).astype(BF16)


def _qkv_proj(a, w_in, layer, cos_t, s1_t, s2_t):
    t, k = a.shape
    bm = _row_block(t, w_in)
    bn = _pick(A_WIDTH, _col_cands(w_in))
    n_cols = 3 * A_WIDTH
    return pl.pallas_call(
        functools.partial(_qkv_kernel, nb_region=A_WIDTH // bn, bn=bn, q_scale=A_HEAD_DIM ** -0.5 * math.log2(math.e)),
        grid=(t // bm, n_cols // bn),
        in_specs=[pl.BlockSpec((bm, k), lambda i, j: (i, 0)),
                  _wspec(w_in, layer, bn, 0),
                  pl.BlockSpec((bm, LANES), lambda i, j: (i, 0)),
                  pl.BlockSpec((bm, LANES), lambda i, j: (i, 0)),
                  pl.BlockSpec((bm, LANES), lambda i, j: (i, 0))],
        out_specs=pl.BlockSpec((bm, bn), lambda i, j: (i, j)),
        out_shape=jax.ShapeDtypeStruct((t, n_cols), BF16),
        compiler_params=_cparams(("parallel", "arbitrary"),
                                 _vmem([((bm, k), BF16), ((k, bn), w_in.dtype), ((bm, bn), F32),
                                        ((bm, bn), F32)])),
        name="qkv_rope_proj",
    )(a, w_in, cos_t, s1_t, s2_t)


def _glu_kernel(x_ref, wa_ref, wg_ref, o_ref):
    x = x_ref[...]
    a = jnp.dot(x, wa_ref[...].astype(BF16), preferred_element_type=F32)
    g = jnp.dot(x, wg_ref[...].astype(BF16), preferred_element_type=F32)
    o_ref[...] = (a * _sigmoid(g)).astype(o_ref.dtype)


def _swiglu_up_kernel(x_ref, w1_ref, w3_ref, o_ref):
    x = x_ref[...]
    a = jnp.dot(x, w1_ref[...].astype(BF16), preferred_element_type=F32)
    b = jnp.dot(x, w3_ref[...].astype(BF16), preferred_element_type=F32)
    o_ref[...] = (a * _sigmoid(a) * b).astype(o_ref.dtype)


def _mm_pair(kern, x, wa, wg, out_dtype, *, off_a=0, off_g=0, n_cols=None, layer=0, name="matmul_pair"):
    t, k = x.shape
    n_cols = wa.shape[-1] if n_cols is None else n_cols
    bm = _row_block(t, wa)
    g = n_cols
    for o in (off_a, off_g):
        g = math.gcd(g, o) if o else g
    bn = _pick(g, _col_cands(wa))
    oa, og = off_a // bn, off_g // bn
    return pl.pallas_call(
        kern,
        grid=(t // bm, n_cols // bn),
        in_specs=[pl.BlockSpec((bm, k), lambda i, j: (i, 0)),
                  _wspec(wa, layer, bn, oa),
                  _wspec(wg, layer, bn, og)],
        out_specs=pl.BlockSpec((bm, bn), lambda i, j: (i, j)),
        out_shape=jax.ShapeDtypeStruct((t, n_cols), out_dtype),
        compiler_params=_cparams(("parallel", "arbitrary"),
                                 _vmem([((bm, k), BF16), ((k, bn), wa.dtype), ((k, bn), wg.dtype), ((bm, bn), F32),
                                        ((bm, bn), F32)])),
        name=name,
    )(x, wa, wg)


def _resid_kernel(*refs, n_ops, bm, n_lat):
    h_ref, gate_ref, o_ref = refs[2 * n_ops:2 * n_ops + 3]
    acc = None
    for k in range(n_ops):
        part = jnp.dot(refs[2 * k][...], refs[2 * k + 1][...].astype(BF16), preferred_element_type=F32)
        acc = part if acc is None else acc + part
    is_lat = _row_is_lat(pl.program_id(0), bm, n_lat)
    gate = jnp.where(is_lat, gate_ref[0:1, :], gate_ref[1:2, :])
    o_ref[...] = h_ref[...] + gate * acc


def _mm_resid(ops, h, mods, k_gate, n_lat, *, bn_cands=None, name="matmul_resid"):
    t, d = h.shape
    bm = _pick(t, (768, 512, 256, 128))
    bn = _pick(d, bn_cands or _col_cands(ops[0][3]))
    nbd = d // bn
    in_specs, args, blocks = [], [], []
    for x, cb, k, w, layer, rb in ops:
        in_specs += [pl.BlockSpec((bm, k), lambda i, j, cb=cb: (i, cb)),
                     pl.BlockSpec((None, k, bn), lambda i, j, layer=layer, rb=rb: (layer, rb, j))]
        args += [x, w]
        blocks += [((bm, k), BF16), ((k, bn), w.dtype)]
    in_specs += [pl.BlockSpec((bm, bn), lambda i, j: (i, j)),
                 pl.BlockSpec((SUBLANES, bn), lambda i, j: (0, k_gate * nbd + j))]
    args += [h, mods]
    blocks += [((bm, bn), F32), ((bm, bn), F32), ((bm, bn), F32)]
    return pl.pallas_call(
        functools.partial(_resid_kernel, n_ops=len(ops), bm=bm, n_lat=n_lat),
        grid=(t // bm, d // bn),
        in_specs=in_specs,
        out_specs=pl.BlockSpec((bm, bn), lambda i, j: (i, j)),
        out_shape=jax.ShapeDtypeStruct((t, d), F32),
        compiler_params=_cparams(("parallel", "arbitrary"), _vmem(blocks)),
        name=name,
    )(*args)


def _attn_kernel(lam_ref, subw_ref, q_ref, k_ref, v_ref, o_ref, s_scr, p_scr, a_scr, acc_scr, m_scr,
                 *, bq, bk, n_lat, n_ctx, lam_init, latent):
    q = q_ref[...]
    lane = lax.broadcasted_iota(jnp.int32, (bq, LANES), 1)
    zero = jnp.zeros_like(q)
    qm = jnp.concatenate([jnp.where(lane < A_HEAD_DIM, q, zero),
                          jnp.where(lane >= A_HEAD_DIM, q, zero)], axis=0)

    n_chunks = n_lat // bk
    rg = 64

    def scores(slot, kc):
        w = kc.shape[0]
        s_scr[slot, :, 0:w] = lax.dot_general(qm, kc, (((1,), (1,)), ((), ())), preferred_element_type=F32)

    def softmax(slot, w):
        for r0 in range(0, 2 * bq, rg):
            s = s_scr[slot, r0:r0 + rg, 0:w]
            m_old = m_scr[r0:r0 + rg, :]
            m_new = jnp.maximum(m_old, jnp.max(s, axis=1, keepdims=True))
            m_scr[r0:r0 + rg, :] = m_new
            a_scr[slot, r0:r0 + rg, :] = jnp.exp2(m_old - m_new)
            p_scr[slot, r0:r0 + rg, 0:w] = jnp.exp2(s - m_new).astype(BF16)

    def accumulate(slot, w, vc):
        vext = jnp.concatenate([vc, jnp.ones_like(vc)], axis=1)
        acc_scr[...] = (a_scr[slot] * acc_scr[...]
                        + jnp.dot(p_scr[slot, :, 0:w], vext, preferred_element_type=F32))

    def kchunk(c):
        return k_ref[pl.ds(pl.multiple_of(c * bk, bk), bk), :]

    def vchunk(c):
        return v_ref[pl.ds(pl.multiple_of(c * bk, bk), bk), :]

    m_scr[...] = jnp.full(m_scr.shape, NEG_BIG, F32)
    acc_scr[...] = jnp.zeros(acc_scr.shape, F32)
    if not latent:
        scores(0, k_ref[...])
        softmax(0, n_ctx)
        accumulate(0, n_ctx, v_ref[...])
    else:
        scores(1, k_ref[n_lat:n_lat + n_ctx, :])
        scores(0, k_ref[0:bk, :])
        softmax(1, n_ctx)
        accumulate(1, n_ctx, v_ref[n_lat:n_lat + n_ctx, :])
        softmax(0, bk)
        scores(1, k_ref[bk:2 * bk, :])

        def body(j, carry):
            accumulate(0, bk, vchunk(2 * j))
            softmax(1, bk)
            scores(0, kchunk(2 * j + 2))
            accumulate(1, bk, vchunk(2 * j + 1))
            softmax(0, bk)
            scores(1, kchunk(2 * j + 3))
            return carry

        lax.fori_loop(0, n_chunks // 2 - 1, body, 0)
        accumulate(0, bk, v_ref[(n_chunks - 2) * bk:(n_chunks - 1) * bk, :])
        softmax(1, bk)
        accumulate(1, bk, v_ref[(n_chunks - 1) * bk:n_chunks * bk, :])

    acc = acc_scr[...]
    lam = lam_ref[...]
    lam_full = (jnp.exp(jnp.sum(lam[0:1] * lam[1:2], axis=1, keepdims=True))
                - jnp.exp(jnp.sum(lam[2:3] * lam[3:4], axis=1, keepdims=True)) + lam_init)
    o = acc[:bq, :LANES] / acc[:bq, LANES:] - lam_full * (acc[bq:, :LANES] / acc[bq:, LANES:])
    y = o * lax.rsqrt(jnp.mean(o * o, axis=-1, keepdims=True) + 1e-5) * subw_ref[...]
    o_ref[...] = (y * (1.0 - lam_init)).astype(BF16)


def _attention(kvq, lam, subw, n_lat, n_ctx, lam_init):
    t = kvq.shape[0]
    bk = _pick(n_lat // 2, (512, 256, 128))
    nh = A_HEADS
    nb_ctx = n_lat // n_ctx

    def call(latent, bq, n_rows, q_map, kv_rows, k_map, v_map):
        return pl.pallas_call(
            functools.partial(_attn_kernel, bq=bq, bk=bk, n_lat=n_lat, n_ctx=n_ctx, lam_init=lam_init,
                              latent=latent),
            grid=(nh, n_rows // bq),
            in_specs=[pl.BlockSpec((4, A_HEAD_DIM), lambda h, i: (0, 0)),
                      pl.BlockSpec((1, LANES), lambda h, i: (0, 0)),
                      pl.BlockSpec((bq, LANES), q_map),
                      pl.BlockSpec((kv_rows, LANES), k_map),
                      pl.BlockSpec((kv_rows, LANES), v_map)],
            out_specs=pl.BlockSpec((bq, LANES), lambda h, i: (i, h)),
            out_shape=jax.ShapeDtypeStruct((n_rows, A_WIDTH), BF16),
            scratch_shapes=[pltpu.VMEM((2, 2 * bq, bk), F32), pltpu.VMEM((2, 2 * bq, bk), BF16),
                            pltpu.VMEM((2, 2 * bq, 1), F32), pltpu.VMEM((2 * bq, 2 * LANES), F32),
                            pltpu.VMEM((2 * bq, 1), F32)],
            compiler_params=_cparams(("parallel", "arbitrary"),
                                     _vmem([((kv_rows, LANES), BF16), ((kv_rows, LANES), BF16)],
                                           scratch=24 << 20)),
            name="diff_attention" if latent else "diff_attention_ctx",
        )(lam, subw.reshape(1, LANES), kvq, kvq, kvq)

    bq = _pick(n_lat, (512, 256, 128))
    att_lat = call(True, bq, n_lat, lambda h, i: (i, 2 * nh + h), t, lambda h, i: (0, h), lambda h, i: (0, nh + h))
    att_ctx = call(False, n_ctx, n_ctx, lambda h, i: (nb_ctx, 2 * nh + h), n_ctx,
                   lambda h, i: (nb_ctx, h), lambda h, i: (nb_ctx, nh + h))
    return jnp.concatenate([att_lat, att_ctx], axis=0)


def _dwconv_shifts(taps):
    base = HALO - (taps - 1) // 2
    return base, sorted({(base + k) % SUBLANES for k in range(taps)} - {0})


def _dwconv_kernel(prev_ref, cur_ref, next_ref, w_ref, b_ref, lnw_ref, lnb_ref, o_ref, xx_ref, sh_ref, y_ref,
                   *, bm, cb, taps, nb_lat, nb_all, layernorm):
    i = pl.program_id(0)
    has_prev = jnp.logical_and(i != 0, i != nb_lat)
    has_next = jnp.logical_and(i != nb_lat - 1, i != nb_all - 1)
    xx_ref[0:HALO, :] = jnp.where(has_prev, prev_ref[...], 0.0)
    xx_ref[HALO:HALO + bm, :] = cur_ref[...]
    xx_ref[HALO + bm:2 * HALO + bm, :] = jnp.where(has_next, next_ref[...], 0.0)
    base, shifts = _dwconv_shifts(taps)
    nr = bm + 2 * HALO - SUBLANES
    rs = min(bm, 64)
    cs = min(cb, 512)
    for c0 in range(0, cb, cs):
        for n, b in enumerate(shifts):
            sh_ref[n, 0:nr, c0:c0 + cs] = xx_ref[b:b + nr, c0:c0 + cs]
        wts = [w_ref[k:k + 1, c0:c0 + cs] for k in range(taps)]
        bias = b_ref[:, c0:c0 + cs]
        for r0 in range(0, bm, rs):
            acc = jnp.broadcast_to(bias, (rs, cs))
            for k in range(taps):
                b = (base + k) % SUBLANES
                lo = r0 + base + k - b
                if b == 0:
                    win = xx_ref[lo:lo + rs, c0:c0 + cs]
                else:
                    win = sh_ref[shifts.index(b), lo:lo + rs, c0:c0 + cs]
                acc = acc + win * wts[k]
            y_ref[r0:r0 + rs, c0:c0 + cs] = acc
    y = y_ref[...]
    if layernorm:
        mu = jnp.mean(y, axis=-1, keepdims=True)
        var = jnp.mean(jnp.square(y - mu), axis=-1, keepdims=True)
        y = (y - mu) * lax.rsqrt(var + LN_EPS) * lnw_ref[...] + lnb_ref[...]
    o_ref[...] = (y * _sigmoid(y)).astype(o_ref.dtype)


def _dwconv_silu(x, w, b, n_lat, out_dtype, *, col_off=0, n_cols=None, ln=None):
    t = x.shape[0]
    taps = w.shape[0]
    n_cols = x.shape[1] - col_off if n_cols is None else n_cols
    bm = _pick(math.gcd(n_lat, t - n_lat), (256, 128))
    cb = n_cols if ln is not None else _pick(math.gcd(n_cols, col_off) if col_off else n_cols, (1024, 512, 256, 128))
    ob = col_off // cb
    hb = bm // HALO
    n_halo_blocks = t // HALO
    lnw, lnb = ln if ln is not None else (jnp.ones((n_cols,), F32), jnp.zeros((n_cols,), F32))
    n_shifts = len(_dwconv_shifts(taps)[1])
    return pl.pallas_call(
        functools.partial(_dwconv_kernel, bm=bm, cb=cb, taps=taps, nb_lat=n_lat // bm, nb_all=t // bm,
                          layernorm=ln is not None),
        grid=(t // bm, n_cols // cb),
        in_specs=[pl.BlockSpec((HALO, cb), lambda i, j: (jnp.maximum(i * hb - 1, 0), j + ob)),
                  pl.BlockSpec((bm, cb), lambda i, j: (i, j + ob)),
                  pl.BlockSpec((HALO, cb), lambda i, j: (jnp.minimum((i + 1) * hb, n_halo_blocks - 1), j + ob)),
                  pl.BlockSpec((taps, cb), lambda i, j: (0, j)),
                  pl.BlockSpec((1, cb), lambda i, j: (0, j)),
                  pl.BlockSpec((1, cb), lambda i, j: (0, j)),
                  pl.BlockSpec((1, cb), lambda i, j: (0, j))],
        out_specs=pl.BlockSpec((bm, cb), lambda i, j: (i, j)),
        out_shape=jax.ShapeDtypeStruct((t, n_cols), out_dtype),
        scratch_shapes=[pltpu.VMEM((bm + 2 * HALO, cb), F32),
                        pltpu.VMEM((n_shifts, bm + 2 * HALO - SUBLANES, cb), F32),
                        pltpu.VMEM((bm, cb), F32)],
        compiler_params=_cparams(("parallel", "parallel"),
                                 _vmem([((bm, cb), F32), ((bm, cb), F32)],
                                       scratch=(n_shifts + 2) * (bm + 2 * HALO) * cb * 4)),
        name="dwconv_silu",
    )(x, x, x, w, b.reshape(1, n_cols), lnw.reshape(1, n_cols), lnb.reshape(1, n_cols))


def _s5_build_kernel(u_ref, lam_ref, step_ref, bt_ref, c_ref, yz_ref, et_ref, adv_ref):
    L, H, P = S5_CHUNK, S5_GROUP, S5_STATE
    lane = lax.broadcasted_iota(jnp.int32, (1, 2 * P), 1)
    first = lane < P
    sgn = jnp.where(first, -1.0, 1.0).astype(F32)
    conj_sgn = -sgn

    def swap(x):
        return pltpu.roll(x, P, axis=1)

    def dupr(x):
        return jnp.where(first, x, swap(x))

    def dupi(x):
        return jnp.where(first, swap(x), x)

    def cmul(a, b):
        return dupr(a) * b + (sgn * dupi(a)) * swap(b)

    one = jnp.where(first, 1.0, 0.0).astype(F32)
    mats, ets, advs = [], [], []
    for d in range(2):
        lam = lam_ref[0, d:d + 1, :]
        z = lam * jnp.exp(step_ref[0, d:d + 1, :])
        ang = dupi(z)
        lbar = jnp.exp(dupr(z)) * jnp.where(first, jnp.cos(ang), jnp.sin(ang))
        den = dupr(lam) * dupr(lam) + dupi(lam) * dupi(lam)
        qcoef = cmul(lbar - one, lam * conj_sgn) / den
        bbar = cmul(qcoef, bt_ref[0])
        cmat = c_ref[0, d * H:(d + 1) * H, :]
        pw = [one]
        for _ in range(L):
            pw.append(cmul(pw[-1], lbar))
        cpow = [cmul(pw[t], cmat) * conj_sgn for t in range(L + 1)]
        if d == 0:
            kern_rows = jnp.concatenate([cpow[t] for t in range(L)], axis=0)
            inj = jnp.concatenate([cmul(pw[L - 1 - j], bbar) for j in range(L)], axis=0)
            read = jnp.concatenate([cpow[t + 1] for t in range(L)], axis=0)
        else:
            kern_rows = jnp.concatenate([cpow[L - 1 - t] for t in range(L)], axis=0)
            inj = jnp.concatenate([cmul(pw[j], bbar) for j in range(L)], axis=0)
            read = jnp.concatenate([cpow[L - t] for t in range(L)], axis=0)
        base = lax.dot_general(bbar.astype(BF16), kern_rows.astype(BF16), (((1,), (1,)), ((), ())),
                               preferred_element_type=F32)
        lane_o = lax.broadcasted_iota(jnp.int32, (H, L * H), 1)
        blocks = []
        for j in range(L):
            if d == 0:
                sh = (j * H) % (L * H)
                blk = jnp.where(lane_o >= j * H, pltpu.roll(base, sh, axis=1) if sh else base, 0.0)
            else:
                sh = (L * H - (L - 1 - j) * H) % (L * H)
                blk = jnp.where(lane_o < (j + 1) * H, pltpu.roll(base, sh, axis=1) if sh else base, 0.0)
            blocks.append(blk)
        mats.append((jnp.concatenate(blocks, axis=0), inj))
        ets.append(read)
        pl_ = pw[L]
        advs.append((dupr(pl_), sgn * dupi(pl_)))
    mix = (mats[0][0] + mats[1][0]).astype(BF16)
    rhs = jnp.concatenate([mix, mats[0][1].astype(BF16), mats[1][1].astype(BF16)], axis=1)
    yz_ref[0] = jnp.dot(u_ref[0], rhs, preferred_element_type=F32)
    et_ref[0] = jnp.concatenate([ets[0], ets[1]], axis=1).astype(BF16)
    adv_ref[0] = jnp.concatenate([jnp.concatenate([advs[0][0], advs[1][0]], axis=1),
                                  jnp.concatenate([advs[0][1], advs[1][1]], axis=1)], axis=0)


def _s5_scan_kernel(z_ref, adv_ref, s_ref, *, n_chunks, nc_ctx):
    P = S5_STATE
    a1 = adv_ref[:, 0, :]
    a2 = adv_ref[:, 1, :]
    gb = a1.shape[0]
    nl = n_chunks - nc_ctx

    def swap_halves(x):
        return jnp.concatenate([pltpu.roll(x[:, :2 * P], P, axis=1), pltpu.roll(x[:, 2 * P:], P, axis=1)], axis=1)

    def body(i, carry):
        s, ssw = carry
        cf = jnp.where(i < nc_ctx, nl + i, i - nc_ctx)
        cb = n_chunks - 1 - i
        z = jnp.concatenate([z_ref[cf][:, :2 * P], z_ref[cb][:, 2 * P:]], axis=1)
        s_ref[cf, :, 0:2 * P] = s[:, :2 * P].astype(s_ref.dtype)
        s_ref[cb, :, 2 * P:4 * P] = s[:, 2 * P:].astype(s_ref.dtype)
        return a1 * s + a2 * ssw + z, a1 * ssw - a2 * s + swap_halves(z)

    zero = jnp.zeros((gb, 4 * P), F32)
    lax.fori_loop(0, n_chunks, body, (zero, zero), unroll=4)


def _s5_out_kernel(yz_ref, s_ref, et_ref, o_ref):
    n = o_ref.shape[2]
    o_ref[0] = yz_ref[0, :, 0:n] + lax.dot_general(s_ref[0], et_ref[0], (((1,), (1,)), ((), ())),
                                                  preferred_element_type=F32)


def _s5_finish_kernel(y_ref, u_ref, d_ref, w_ref, b_ref, o_ref):
    y = y_ref[...] + d_ref[...] * u_ref[...]
    g = 0.5 * y * (1.0 + jnp.tanh(math.sqrt(2.0 / math.pi) * (y + 0.044715 * (y * y * y))))
    gate = jnp.dot(g.astype(BF16), w_ref[...], preferred_element_type=F32) + b_ref[...]
    o_ref[...] = (g * _sigmoid(gate)).astype(BF16)


def _perm_kernel(x_ref, w_ref, o_ref):
    x = x_ref[...]
    w = w_ref[...]
    hi = x.astype(BF16)
    acc = jnp.dot(hi, w, preferred_element_type=F32)
    if x.dtype == F32:
        lo = (x - hi.astype(F32)).astype(BF16)
        acc = acc + jnp.dot(lo, w, preferred_element_type=F32)
    o_ref[...] = acc.astype(o_ref.dtype)


def _permute_lanes(x, pmat, out_dtype):
    r, n = x.shape
    bm = _pick(r, (1408, 768, 512, 256, 128, 64, 32, 16, 8))
    bn = _pick(n, (512, 256, 128))
    return pl.pallas_call(
        _perm_kernel,
        grid=(r // bm, n // bn),
        in_specs=[pl.BlockSpec((bm, n), lambda i, j: (i, 0)), pl.BlockSpec((n, bn), lambda i, j: (0, j))],
        out_specs=pl.BlockSpec((bm, bn), lambda i, j: (i, j)),
        out_shape=jax.ShapeDtypeStruct((r, n), out_dtype),
        compiler_params=_cparams(("parallel", "arbitrary"),
                                 _vmem([((bm, n), x.dtype), ((n, bn), BF16), ((bm, bn), F32), ((bm, n), BF16)])),
        name="lane_permute",
    )(x, pmat)


def _s5_branch(u, lam_re, lam_im, log_step, b_re, b_im, c_re, c_im, d_skip, glu_w, glu_b, n_lat):
    t = u.shape[0]
    G, H, P, L = S5_GROUPS, S5_GROUP, S5_STATE, S5_CHUNK
    nch = t // L
    nc_ctx = (t - n_lat) // L
    ntile, gpt = (G * H) // LANES, LANES // H
    src = jnp.arange(L * LANES)
    dst = ((src % LANES) // H) * (L * H) + (src // LANES) * H + src % H
    to_group = (dst[:, None] == src[None, :]).astype(BF16)
    to_token = (src[:, None] == dst[None, :]).astype(BF16)
    lam_p = jnp.concatenate([lam_re, lam_im], axis=-1).transpose(1, 0, 2)
    step_p = jnp.broadcast_to(log_step.T[:, :, None], (G, 2, 2 * P))
    bt_p = jnp.concatenate([b_re, b_im], axis=1).transpose(0, 2, 1)
    c_p = jnp.concatenate([c_re, c_im], axis=-1).transpose(1, 0, 2, 3).reshape(G, 2 * H, 2 * P)
    LH = L * H
    ut = u.astype(BF16).reshape(nch, L, ntile, LANES).transpose(2, 0, 1, 3).reshape(ntile * nch, L * LANES)
    ug = (_permute_lanes(ut, to_group, BF16).reshape(ntile, nch, gpt, LH).transpose(0, 2, 1, 3)
          .reshape(G, nch, LH))
    yz, et, adv = pl.pallas_call(
        _s5_build_kernel,
        grid=(G,),
        in_specs=[pl.BlockSpec((1, nch, LH), lambda g: (g, 0, 0)),
                  pl.BlockSpec((1, 2, 2 * P), lambda g: (g, 0, 0)),
                  pl.BlockSpec((1, 2, 2 * P), lambda g: (g, 0, 0)),
                  pl.BlockSpec((1, H, 2 * P), lambda g: (g, 0, 0)),
                  pl.BlockSpec((1, 2 * H, 2 * P), lambda g: (g, 0, 0))],
        out_specs=[pl.BlockSpec((1, nch, LH + 4 * P), lambda g: (g, 0, 0)),
                   pl.BlockSpec((1, LH, 4 * P), lambda g: (g, 0, 0)),
                   pl.BlockSpec((1, 2, 4 * P), lambda g: (g, 0, 0))],
        out_shape=[jax.ShapeDtypeStruct((G, nch, LH + 4 * P), F32),
                   jax.ShapeDtypeStruct((G, LH, 4 * P), BF16),
                   jax.ShapeDtypeStruct((G, 2, 4 * P), F32)],
        compiler_params=_cparams(("parallel",), _vmem([((nch, LH), BF16), ((nch, LH + 4 * P), F32)])),
        name="s5_build_apply",
    )(ug, lam_p, step_p, bt_p, c_p)
    z_t = yz[:, :, LH:].transpose(1, 0, 2)
    gb = _pick(G, (16, 8))
    s_in = pl.pallas_call(
        functools.partial(_s5_scan_kernel, n_chunks=nch, nc_ctx=nc_ctx),
        grid=(G // gb,),
        in_specs=[pl.BlockSpec((nch, gb, 4 * P), lambda g: (0, g, 0)),
                  pl.BlockSpec((gb, 2, 4 * P), lambda g: (g, 0, 0))],
        out_specs=pl.BlockSpec((nch, gb, 4 * P), lambda g: (0, g, 0)),
        out_shape=jax.ShapeDtypeStruct((nch, G, 4 * P), BF16),
        compiler_params=_cparams(("parallel",), _vmem([((nch, gb, 4 * P), F32), ((nch, gb, 4 * P), BF16)])),
        name="s5_chunk_scan",
    )(z_t, adv)
    s_g = s_in.transpose(1, 0, 2)
    y = pl.pallas_call(
        _s5_out_kernel,
        grid=(G,),
        in_specs=[pl.BlockSpec((1, nch, LH + 4 * P), lambda g: (g, 0, 0)),
                  pl.BlockSpec((1, nch, 4 * P), lambda g: (g, 0, 0)),
                  pl.BlockSpec((1, LH, 4 * P), lambda g: (g, 0, 0))],
        out_specs=pl.BlockSpec((1, nch, LH), lambda g: (g, 0, 0)),
        out_shape=jax.ShapeDtypeStruct((G, nch, LH), F32),
        compiler_params=_cparams(("parallel",), _vmem([((nch, LH + 4 * P), F32), ((nch, 4 * P), BF16),
                                                       ((nch, LH), F32)])),
        name="s5_readout",
    )(yz, s_g, et)
    yt = y.reshape(ntile, gpt, nch, LH).transpose(0, 2, 1, 3).reshape(ntile * nch, gpt * LH)
    y_tok = (_permute_lanes(yt, to_token, F32).reshape(ntile, nch, L, LANES).transpose(1, 2, 0, 3)
             .reshape(t, G * H))
    w = G * H
    bm = _pick(t, (256, 128))
    return pl.pallas_call(
        _s5_finish_kernel,
        grid=(t // bm,),
        in_specs=[pl.BlockSpec((bm, w), lambda i: (i, 0)),
                  pl.BlockSpec((bm, w), lambda i: (i, 0)),
                  pl.BlockSpec((1, w), lambda i: (0, 0)),
                  pl.BlockSpec((w, w), lambda i: (0, 0)),
                  pl.BlockSpec((1, w), lambda i: (0, 0))],
        out_specs=pl.BlockSpec((bm, w), lambda i: (i, 0)),
        out_shape=jax.ShapeDtypeStruct((t, w), BF16),
        compiler_params=_cparams(("parallel",), _vmem([((bm, w), F32), ((bm, w), F32), ((w, w), BF16),
                                                       ((bm, w), F32)])),
        name="s5_gelu_glu",
    )(y_tok, u, d_skip.reshape(1, w), glu_w.astype(BF16), glu_b.reshape(1, w))


def _ssd_chunk(forward, xs_ref, b_ref, c_ref, dt_ref, sel_ref, bias_ref, alog_ref, y_ref, state_ref, hpg):
    L = SSD_CHUNK
    hd = M2_HEAD_DIM
    sel = sel_ref[0]
    hi, mid, lo = _split3(dt_ref[...])
    dt_raw = (jnp.dot(hi, sel, preferred_element_type=F32) + jnp.dot(mid, sel, preferred_element_type=F32)
              + jnp.dot(lo, sel, preferred_element_type=F32))
    v = dt_raw + bias_ref[0]
    dt = jnp.maximum(v, 0.0) + jnp.log1p(jnp.exp(-jnp.abs(v)))
    da = dt * (-jnp.exp(alog_ref[0]))
    ti = lax.broadcasted_iota(jnp.int32, (L, L), 0)
    si = lax.broadcasted_iota(jnp.int32, (L, L), 1)
    tri = jnp.where(si <= ti, 1.0, 0.0).astype(BF16)
    hi, mid, lo = _split3(da)
    cum = (jnp.dot(tri, hi, preferred_element_type=F32) + jnp.dot(tri, mid, preferred_element_type=F32)
           + jnp.dot(tri, lo, preferred_element_type=F32))
    total = cum[L - 1:L, :]
    if forward:
        cumq = cum
        w_out = jnp.exp(cumq)
        w_state = jnp.exp(total - cumq)
        mask = si <= ti
    else:
        cumq = cum - da
        w_out = jnp.exp(total - cumq)
        w_state = jnp.exp(cumq)
        mask = si >= ti
    xsdt = xs_ref[...] * dt
    cb_ = c_ref[...]
    bb_ = b_ref[...]
    s_old = state_ref[...]
    y_off = w_out * jnp.dot(cb_, s_old.astype(BF16), preferred_element_type=F32)
    state_ref[...] = jnp.exp(total) * s_old + lax.dot_general(
        bb_, (w_state * xsdt).astype(BF16), (((0,), (0,)), ((), ())), preferred_element_type=F32)
    cbm = lax.dot_general(cb_, bb_, (((1,), (1,)), ((), ())), preferred_element_type=F32)
    cum_t = cumq.T
    lane = lax.broadcasted_iota(jnp.int32, (L, LANES), 1)
    xb = xsdt.astype(BF16)
    zero = jnp.zeros((L, LANES), BF16)
    outs = []
    for pr in range(hpg // 2):
        gs = []
        for r in (2 * pr, 2 * pr + 1):
            col = cumq[:, hd * r:hd * r + 1]
            row = cum_t[hd * r:hd * r + 1, :]
            diff = (col - row) if forward else (row - col)
            decay = jnp.exp(jnp.where(mask, diff, NEG_BIG))
            gs.append((cbm * decay).astype(BF16))
        xp = xb[:, LANES * pr:LANES * (pr + 1)]
        rhs = jnp.concatenate([jnp.where(lane < hd, xp, zero), jnp.where(lane >= hd, xp, zero)], axis=0)
        outs.append(jnp.dot(jnp.concatenate(gs, axis=1), rhs, preferred_element_type=F32))
    y_ref[...] = jnp.concatenate(outs, axis=1) + y_off


def _ssd_kernel(*refs, hpg, gpb):
    fwd, bwd, (yf_ref, yb_ref, sf_ref, sb_ref) = refs[0:7], refs[7:14], refs[14:18]
    w = hpg * M2_HEAD_DIM

    @pl.when(pl.program_id(1) == 0)
    def _():
        sf_ref[...] = jnp.zeros_like(sf_ref)
        sb_ref[...] = jnp.zeros_like(sb_ref)

    for forward, ins, y_ref, s_ref in ((True, fwd, yf_ref, sf_ref), (False, bwd, yb_ref, sb_ref)):
        xs_ref, b_ref, c_ref, dt_ref, sel_ref, bias_ref, alog_ref = ins
        for k in range(gpb):
            cols = slice(k * w, (k + 1) * w)
            st = slice(k * M2_STATE, (k + 1) * M2_STATE)
            _ssd_chunk(forward, xs_ref.at[:, cols], b_ref.at[:, st], c_ref.at[:, st], dt_ref,
                       sel_ref.at[k:k + 1], bias_ref.at[k:k + 1], alog_ref.at[k:k + 1],
                       y_ref.at[:, cols], s_ref.at[k], hpg)


def _ssd_finish_kernel(yf_ref, yb_ref, xs_ref, z_ref, d_ref, nw_ref, o_ref, *, n_groups):
    y = yf_ref[...] + yb_ref[...] + d_ref[...] * xs_ref[...]
    z = z_ref[...]
    y = y * (z * _sigmoid(z))
    gw = y.shape[1] // n_groups
    parts = []
    for g in range(n_groups):
        seg = y[:, g * gw:(g + 1) * gw]
        parts.append(seg * lax.rsqrt(jnp.mean(seg * seg, axis=-1, keepdims=True) + RMS_EPS))
    o_ref[...] = (jnp.concatenate(parts, axis=1) * nw_ref[...]).astype(BF16)


def _ssd_branch(xs, bm_, cm_, dt_raw, z, a_log, dt_bias, d_skip, norm_w, n_lat):
    t, inner = xs.shape
    L = SSD_CHUNK
    ng = M2_GROUPS
    hpg = M2_HEADS // ng
    w = hpg * M2_HEAD_DIM
    nch = t // L
    nc_ctx = (t - n_lat) // L
    nl = nch - nc_ctx
    src = (jnp.arange(2)[:, None, None] * M2_HEADS + jnp.arange(ng)[None, :, None] * hpg
           + jnp.arange(w)[None, None, :] // M2_HEAD_DIM)
    sel = (jnp.arange(LANES)[None, None, :, None] == src[:, :, None, :]).astype(BF16)
    expand = lambda p: jnp.repeat(p, M2_HEAD_DIM, axis=-1).reshape(2, ng, 1, w)
    bias_e = expand(dt_bias)
    alog_e = expand(a_log)

    order = (lambda c: jnp.where(c < nc_ctx, nl + c, c - nc_ctx), lambda c: nch - 1 - c)
    gpb = 2 if ng % 2 == 0 else 1
    in_specs, args = [], []
    for d in range(2):
        ch = order[d]
        in_specs += [pl.BlockSpec((L, gpb * w), lambda g, c, ch=ch: (ch(c), g)),
                     pl.BlockSpec((L, gpb * M2_STATE), lambda g, c, ch=ch: (ch(c), g)),
                     pl.BlockSpec((L, gpb * M2_STATE), lambda g, c, ch=ch: (ch(c), g)),
                     pl.BlockSpec((L, LANES), lambda g, c, ch=ch: (ch(c), 0)),
                     pl.BlockSpec((gpb, LANES, w), lambda g, c: (g, 0, 0)),
                     pl.BlockSpec((gpb, 1, w), lambda g, c: (g, 0, 0)),
                     pl.BlockSpec((gpb, 1, w), lambda g, c: (g, 0, 0))]
        args += [xs, bm_, cm_, dt_raw, sel[d], bias_e[d], alog_e[d]]
    yf, yb = pl.pallas_call(
        functools.partial(_ssd_kernel, hpg=hpg, gpb=gpb),
        grid=(ng // gpb, nch),
        in_specs=in_specs,
        out_specs=[pl.BlockSpec((L, gpb * w), lambda g, c: (order[0](c), g)),
                   pl.BlockSpec((L, gpb * w), lambda g, c: (order[1](c), g))],
        out_shape=[jax.ShapeDtypeStruct((t, inner), F32), jax.ShapeDtypeStruct((t, inner), F32)],
        scratch_shapes=[pltpu.VMEM((gpb, M2_STATE, w), F32), pltpu.VMEM((gpb, M2_STATE, w), F32)],
        compiler_params=_cparams(("parallel", "arbitrary"), 32 << 20),
        name="ssd_chunk_scan",
    )(*args)
    bm = _pick(t, (256, 128))
    d_e = jnp.repeat(d_skip, M2_HEAD_DIM).reshape(1, inner)
    return pl.pallas_call(
        functools.partial(_ssd_finish_kernel, n_groups=ng),
        grid=(t // bm,),
        in_specs=[pl.BlockSpec((bm, inner), lambda i: (i, 0)),
                  pl.BlockSpec((bm, inner), lambda i: (i, 0)),
                  pl.BlockSpec((bm, inner), lambda i: (i, 0)),
                  pl.BlockSpec((bm, inner), lambda i: (i, 0)),
                  pl.BlockSpec((1, inner), lambda i: (0, 0)),
                  pl.BlockSpec((1, inner), lambda i: (0, 0))],
        out_specs=pl.BlockSpec((bm, inner), lambda i: (i, 0)),
        out_shape=jax.ShapeDtypeStruct((t, inner), BF16),
        compiler_params=_cparams(("parallel",), _vmem([((bm, inner), F32)] * 5)),
        name="ssd_gate_norm",
    )(yf, yb, xs, z, d_e, norm_w.reshape(1, inner))


def _rope_tables(n_lat, n_ctx):
    n_rows = n_lat // GRID_W
    rows = jnp.repeat(jnp.arange(n_rows), GRID_W).astype(F32)
    cols = jnp.tile(jnp.arange(GRID_W), n_rows).astype(F32)
    inv = jnp.power(ROPE_BASE, -jnp.arange(0, ROPE_AXIS_DIM, 2, dtype=F32) / ROPE_AXIS_DIM)
    ang_r = rows[:, None] * inv
    ang_c = cols[:, None] * inv
    ang = jnp.concatenate([ang_r, ang_r, ang_c, ang_c], axis=-1)
    cos, sin = jnp.cos(ang), jnp.sin(ang)
    half = ROPE_AXIS_DIM // 2
    first = (jnp.arange(A_HEAD_DIM) % ROPE_AXIS_DIM) < half
    s1 = jnp.where(first, 0.0, sin)
    s2 = jnp.where(first, -sin, 0.0)
    pad = lambda tbl, fill: jnp.concatenate([tbl, jnp.full((n_ctx, A_HEAD_DIM), fill, F32)], axis=0)
    two = lambda tbl: jnp.concatenate([tbl, tbl], axis=-1)
    return two(pad(cos, 1.0)), two(pad(s1, 0.0)), two(pad(s2, 0.0))


def _swiglu_ffn(h, mods, nw, w1_all, w3_all, layer, w2, n_lat):
    a = _normmod(h, nw, mods, 3, 4, n_lat)
    hid = _mm_pair(_swiglu_up_kernel, a, w1_all, w3_all, BF16, layer=layer, name="ffn_gate_up")
    return _mm_resid([(hid, 0, hid.shape[1], w2, layer, 0)], h, mods, 5, n_lat, bn_cands=(256, 128),
                     name="ffn_down_resid")


def _even_layer(h, a, mods, w_in_all, j, w_out_all, lam, subw, conv_w, conv_b, ln_w, ln_b, rope, layer_idx, n_lat):
    n_ctx = h.shape[0] - n_lat
    kvq = _qkv_proj(a, w_in_all, j, *rope)
    glu = _mm_pair(_glu_kernel, a, w_in_all, w_in_all, F32, off_a=EV_B0, off_g=EV_B0 + B_WIDTH, n_cols=B_WIDTH,
                   layer=j, name="conv_glu_proj")
    lam_init = 0.8 - 0.6 * math.exp(-0.3 * layer_idx)
    att = _attention(kvq, lam, subw, n_lat, n_ctx, lam_init)
    cv = _dwconv_silu(glu, conv_w, conv_b, n_lat, BF16, ln=(ln_w, ln_b))
    assert A_WIDTH == B_WIDTH
    return _mm_resid([(att, 0, A_WIDTH, w_out_all, j, 0), (cv, 0, B_WIDTH, w_out_all, j, 1)], h, mods, 2, n_lat,
                     name="even_out_resid")


def _odd_layer(h, a, mods, w_in_all, j, w_out_all, s5p, m2p, n_lat):
    u = _mm_plain(a, w_in_all, F32, col_off=0, n_cols=S5_WIDTH, layer=j)
    xbc = _mm_plain(a, w_in_all, F32, col_off=S5_WIDTH, n_cols=M2_CONV_DIM, layer=j)
    n_dt = 2 * M2_HEADS
    k_in, n_in = w_in_all.shape[1:]
    w_dt = lax.slice(w_in_all, (j, 0, OD_DT0), (j + 1, k_in, OD_Z0))[0]
    w_dt = jnp.pad(w_dt, ((0, 0), (0, LANES - n_dt))).astype(BF16)
    dt_raw = _mm_plain(a, w_dt, F32)
    z = _mm_plain(a, lax.slice(w_in_all, (j, 0, OD_Z0), (j + 1, k_in, n_in))[0].astype(BF16), F32)
    s5_out = _s5_branch(u, *s5p, n_lat)
    conv_w, conv_b, a_log, dt_bias, m2_d, m2_norm_w = m2p
    gn = M2_GROUPS * M2_STATE
    xs = _dwconv_silu(xbc, conv_w[:, :M2_INNER], conv_b[:M2_INNER], n_lat, F32, col_off=0, n_cols=M2_INNER)
    bmat = _dwconv_silu(xbc, conv_w[:, M2_INNER:M2_INNER + gn], conv_b[M2_INNER:M2_INNER + gn], n_lat, BF16,
                        col_off=M2_INNER, n_cols=gn)
    cmat = _dwconv_silu(xbc, conv_w[:, M2_INNER + gn:], conv_b[M2_INNER + gn:], n_lat, BF16,
                        col_off=M2_INNER + gn, n_cols=gn)
    ssd_out = _ssd_branch(xs, bmat, cmat, dt_raw, z, a_log, dt_bias, m2_d, m2_norm_w, n_lat)
    assert M2_INNER % S5_WIDTH == 0
    ops = [(s5_out, 0, S5_WIDTH, w_out_all, j, 0)]
    ops += [(ssd_out, cb, S5_WIDTH, w_out_all, j, 1 + cb) for cb in range(M2_INNER // S5_WIDTH)]
    return _mm_resid(ops, h, mods, 2, n_lat, name="odd_out_resid")


def kernel(x, c, ctx, c_ctx, ada_w, ada_b, norm_w, ffn_w1, ffn_w3, ffn_w2, ev_w_in, ev_w_out, ev_lambda, ev_subln_w, ev_conv_w, ev_conv_b, ev_ln_w, ev_ln_b, od_w_in, od_w_out, s5_lam_re, s5_lam_im, s5_log_step, s5_b_re, s5_b_im, s5_c_re, s5_c_im, s5_d, s5_glu_w, s5_glu_b, m2_conv_w, m2_conv_b, m2_a_log, m2_dt_bias, m2_d, m2_norm_w, final_norm_w):
    assert x.shape[0] == 1 and c.shape[0] == 1 and ctx.shape[0] == 1
    n_lat, n_ctx = x.shape[1], ctx.shape[1]
    depth = ada_w.shape[0]
    mods_all = _ada_mods(c, c_ctx, ada_w, ada_b)
    rope = _rope_tables(n_lat, n_ctx)
    w2_all = ffn_w2.astype(BF16)
    h = None
    for i in range(depth):
        mods = mods_all[i]
        j = i // 2
        if i == 0:
            a, h = _normmod_first(x[0], ctx[0], norm_w[i, 0], mods, 0, 1)
        else:
            a = _normmod(h, norm_w[i, 0], mods, 0, 1, n_lat)
        if i % 2 == 0:
            h = _even_layer(h, a, mods, ev_w_in, j, ev_w_out, ev_lambda[j], ev_subln_w[j],
                            ev_conv_w[j], ev_conv_b[j], ev_ln_w[j], ev_ln_b[j], rope, i, n_lat)
        else:
            s5p = (s5_lam_re[j], s5_lam_im[j], s5_log_step[j], s5_b_re[j], s5_b_im[j], s5_c_re[j], s5_c_im[j],
                   s5_d[j], s5_glu_w[j], s5_glu_b[j])
            m2p = (m2_conv_w[j], m2_conv_b[j], m2_a_log[j], m2_dt_bias[j], m2_d[j], m2_norm_w[j])
            h = _odd_layer(h, a, mods, od_w_in, j, od_w_out, s5p, m2p, n_lat)
        h = _swiglu_ffn(h, mods, norm_w[i, 1], ffn_w1, ffn_w3, i, w2_all, n_lat)
    return _final_norm(h, final_norm_w, n_lat)[None]
```
